```python
import math
import jax, jax.numpy as jnp
from jax import lax
import numpy as np

D_MODEL = 2048
BATCH = 4
SEQ = 2048
DEPTH = 4
DEC_BATCH = 8
DEC_SEQ = 4
PAST_LEN = 16384
PAGE_SIZE = 128

HEAD_DIM = 128
N_HEADS = D_MODEL // HEAD_DIM
H_A = N_HEADS // 2
H_B = N_HEADS - H_A
D_HALF = HEAD_DIM // 2
IDX_HEADS = 16
IDX_DIM = 64
TOPK_MAX = 256
D_FF = 4 * D_MODEL
N_BUCKETS = 32
MAX_DISTANCE = 128
Q_BLOCK = 128
EPS = 1e-6
NEG = -1e30
D_IN = 3 * H_A * HEAD_DIM + IDX_HEADS * IDX_DIM + IDX_DIM + IDX_HEADS + 3 * H_B * HEAD_DIM

kernel_name = "hymba_dsa_diffattn_decoder_step"

F32 = jnp.float32


def rmsnorm(x, g):
    xf = x.astype(F32)
    y = xf * lax.rsqrt(jnp.mean(xf * xf, axis=-1, keepdims=True) + EPS)
    return (y * g.astype(F32)).astype(x.dtype)


def t5_bucket(dist):
    dist = jnp.maximum(dist, 0)
    max_exact = N_BUCKETS // 2
    d = jnp.maximum(dist, 1).astype(F32)
    large = max_exact + (jnp.log(d / max_exact) / math.log(MAX_DISTANCE / max_exact)
                         * (N_BUCKETS - max_exact)).astype(jnp.int32)
    large = jnp.minimum(large, N_BUCKETS - 1)
    return jnp.where(dist < max_exact, dist, large)


def split_proj(z):
    sizes = (H_A * HEAD_DIM, H_A * HEAD_DIM, H_A * HEAD_DIM, IDX_HEADS * IDX_DIM, IDX_DIM,
             IDX_HEADS, H_B * HEAD_DIM, H_B * HEAD_DIM, H_B * HEAD_DIM)
    offs = [int(o) for o in np.cumsum(sizes)[:-1]]
    qa, ka, va, qi, ki, wi, qb, kb, vb = jnp.split(z, offs, axis=-1)
    lead = z.shape[:-1]
    return (qa.reshape(*lead, H_A, HEAD_DIM), ka.reshape(*lead, H_A, HEAD_DIM),
            va.reshape(*lead, H_A, HEAD_DIM), qi.reshape(*lead, IDX_HEADS, IDX_DIM), ki, wi,
            qb.reshape(*lead, H_B, 2, D_HALF), kb.reshape(*lead, H_B, HEAD_DIM),
            vb.reshape(*lead, H_B, HEAD_DIM))


def indexer_scores(qi, wi, ki, qpos, kpos):
    dots = jnp.einsum('bqhd,bsd->bqhs', qi, ki).astype(F32) * IDX_DIM ** -0.5
    s = jnp.einsum('bqh,bqhs->bqs', wi.astype(F32) * IDX_HEADS ** -0.5, jax.nn.relu(dots))
    return jnp.where((qpos[:, None] >= kpos[None, :])[None], s, NEG)


def dsa_attend(q, k_sel, v_sel, qpos, sel, table):
    logits = jnp.einsum('bqhd,bqkhd->bhqk', q, k_sel).astype(F32) * HEAD_DIM ** -0.5
    dist = qpos[None, :, None] - sel
    logits = logits + jnp.transpose(table[t5_bucket(dist)], (0, 3, 1, 2)).astype(F32)
    logits = jnp.where((dist >= 0)[:, None], logits, NEG)
    p = jax.nn.softmax(logits, axis=-1)
    return jnp.einsum('bhqk,bqkhd->bqhd', p.astype(v_sel.dtype), v_sel).astype(q.dtype)


def diff_attend(q, k, v, qpos, kpos, table, lam, lam_init, g_sub):
    logits = jnp.einsum('bqhmd,bshmd->bhmqs', q, k).astype(F32) * D_HALF ** -0.5
    dist = qpos[:, None] - kpos[None, :]
    bias = jnp.transpose(table[t5_bucket(dist)], (2, 0, 1)).astype(F32)
    logits = jnp.where((dist >= 0)[None, None, None], logits + bias[None, :, None], NEG)
    p = jax.nn.softmax(logits, axis=-1)
    a = p[:, :, 0] - lam * p[:, :, 1]
    o = jnp.einsum('bhqs,bshd->bqhd', a, v.astype(F32))
    o = rmsnorm(o, g_sub) * (1.0 - lam_init)
    return o.astype(q.dtype)


def merge_heads(oa, ob):
    return jnp.concatenate([oa.reshape(*oa.shape[:2], -1), ob.reshape(*ob.shape[:2], -1)], axis=-1)


def sq_relu_mlp(x, w_up, w_down):
    h = jax.nn.relu(x @ w_up)
    return (h * h) @ w_down


def prompt_mix(qa, ka, va, qi, ki, wi, qb, kb, vb, table_a, table_b, lam, lam_init, g_sub, topk):
    B = qa.shape[0]
    nb = SEQ // Q_BLOCK
    kpos = jnp.arange(SEQ, dtype=jnp.int32)
    bidx = jnp.arange(B)[:, None, None]
    kb5 = kb.reshape(B, SEQ, H_B, 2, D_HALF)

    def to_blocks(a):
        return a.reshape(B, nb, Q_BLOCK, *a.shape[2:]).swapaxes(0, 1)

    def block(xs):
        qa_b, qi_b, wi_b, qb_b, qpos = xs
        _, sel = lax.top_k(indexer_scores(qi_b, wi_b, ki, qpos, kpos), topk)
        oa = dsa_attend(qa_b, ka[bidx, sel], va[bidx, sel], qpos, sel, table_a)
        ob = diff_attend(qb_b, kb5, vb, qpos, kpos, table_b, lam, lam_init, g_sub)
        return oa, ob

    xs = (to_blocks(qa), to_blocks(qi), to_blocks(wi), to_blocks(qb), kpos.reshape(nb, Q_BLOCK))
    oa, ob = lax.map(block, xs)
    oa = oa.swapaxes(0, 1).reshape(B, SEQ, H_A, HEAD_DIM)
    ob = ob.swapaxes(0, 1).reshape(B, SEQ, H_B, HEAD_DIM)
    return oa, ob


def sample_mix(l, qa, ka, va, qi, ki, wi, qb, kb, vb, cache_k_a, cache_v_a, cache_k_idx,
               cache_k_b, cache_v_b, page_table, table_a, table_b, lam, lam_init, g_sub, topk):
    Bd = qa.shape[0]
    L = PAST_LEN + DEC_SEQ
    qpos = PAST_LEN + jnp.arange(DEC_SEQ, dtype=jnp.int32)
    kpos = jnp.arange(L, dtype=jnp.int32)
    bidx = jnp.arange(Bd)[:, None, None]

    def past(cache):
        g = cache[l, page_table]
        return g.reshape(Bd, PAST_LEN, *g.shape[3:])

    ki_all = jnp.concatenate([past(cache_k_idx), ki.astype(cache_k_idx.dtype)], axis=1)
    _, sel = lax.top_k(indexer_scores(qi, wi, ki_all, qpos, kpos), topk)
    in_past = sel < PAST_LEN
    ps = jnp.minimum(sel, PAST_LEN - 1)
    phys = page_table[bidx, ps // PAGE_SIZE]
    off = ps % PAGE_SIZE
    nw = jnp.clip(sel - PAST_LEN, 0, DEC_SEQ - 1)

    def gather_sel(cache, new):
        return jnp.where(in_past[..., None, None], cache[l, phys, off], new[bidx, nw].astype(cache.dtype))

    oa = dsa_attend(qa, gather_sel(cache_k_a, ka), gather_sel(cache_v_a, va), qpos, sel, table_a)
    kb_all = jnp.concatenate([past(cache_k_b), kb.astype(cache_k_b.dtype)], axis=1).reshape(Bd, L, H_B, 2, D_HALF)
    vb_all = jnp.concatenate([past(cache_v_b), vb.astype(cache_v_b.dtype)], axis=1)
    ob = diff_attend(qb, kb_all, vb_all, qpos, kpos, table_b, lam, lam_init, g_sub)
    return oa, ob


def setup_inputs(seed: int = 0) -> dict:
    key = jax.random.key(seed)
    ks = jax.random.split(key, 24)
    nrm = jax.random.normal
    n_pages = PAST_LEN // PAGE_SIZE
    n_used = DEC_BATCH * n_pages
    n_pool = n_used + n_used // 4
    page_table = jax.random.permutation(ks[0], n_pool)[:n_used].reshape(DEC_BATCH, n_pages).astype(jnp.int32)
    return {
        "x_prompt": nrm(ks[1], (BATCH, SEQ, D_MODEL), F32),
        "x_sample": nrm(ks[2], (DEC_BATCH, DEC_SEQ, D_MODEL), F32),
        "cache_k_a": nrm(ks[3], (DEPTH, n_pool, PAGE_SIZE, H_A, HEAD_DIM), F32),
        "cache_v_a": nrm(ks[4], (DEPTH, n_pool, PAGE_SIZE, H_A, HEAD_DIM), F32),
        "cache_k_idx": nrm(ks[5], (DEPTH, n_pool, PAGE_SIZE, IDX_DIM), F32),
        "cache_k_b": nrm(ks[6], (DEPTH, n_pool, PAGE_SIZE, H_B, HEAD_DIM), F32),
        "cache_v_b": nrm(ks[7], (DEPTH, n_pool, PAGE_SIZE, H_B, HEAD_DIM), F32),
        "page_table": page_table,
        "w_in": nrm(ks[8], (DEPTH, D_MODEL, D_IN), F32) * D_MODEL ** -0.5,
        "w_out": nrm(ks[9], (DEPTH, N_HEADS * HEAD_DIM, D_MODEL), F32) * (N_HEADS * HEAD_DIM) ** -0.5,
        "w_up": nrm(ks[10], (DEPTH, D_MODEL, D_FF), F32) * D_MODEL ** -0.5,
        "w_down": nrm(ks[11], (DEPTH, D_FF, D_MODEL), F32) * D_FF ** -0.5,
        "g_attn": 1.0 + 0.1 * nrm(ks[12], (DEPTH, D_MODEL), F32),
        "g_mlp": 1.0 + 0.1 * nrm(ks[13], (DEPTH, D_MODEL), F32),
        "g_sub": 1.0 + 0.1 * nrm(ks[14], (DEPTH, HEAD_DIM), F32),
        "lam_q1": 0.1 * nrm(ks[15], (DEPTH, D_HALF), F32),
        "lam_k1": 0.1 * nrm(ks[16], (DEPTH, D_HALF), F32),
        "lam_q2": 0.1 * nrm(ks[17], (DEPTH, D_HALF), F32),
        "lam_k2": 0.1 * nrm(ks[18], (DEPTH, D_HALF), F32),
        "rel_bias": 0.5 * nrm(ks[19], (N_BUCKETS, N_HEADS), F32),
        "g_final": 1.0 + 0.1 * nrm(ks[20], (D_MODEL,), F32),
    }


def reference(x_prompt, x_sample, cache_k_a, cache_v_a, cache_k_idx, cache_k_b, cache_v_b, page_table,
              w_in, w_out, w_up, w_down, g_attn, g_mlp, g_sub, lam_q1, lam_k1, lam_q2, lam_k2,
              rel_bias, g_final):
    table_a = rel_bias[:, :H_A]
    table_b = rel_bias[:, H_A:]
    topk_p = min(TOPK_MAX, SEQ // 4)
    topk_s = min(TOPK_MAX, (PAST_LEN + DEC_SEQ) // 4)
    hp, hs = x_prompt, x_sample
    rows_p, rows_s = [], []
    for l in range(DEPTH):
        lam_init = 0.8 - 0.6 * math.exp(-0.3 * l)
        lam = (jnp.exp(jnp.sum(lam_q1[l].astype(F32) * lam_k1[l].astype(F32)))
               - jnp.exp(jnp.sum(lam_q2[l].astype(F32) * lam_k2[l].astype(F32))) + lam_init)
        qa, ka, va, qi, ki, wi, qb, kb, vb = split_proj(rmsnorm(hp, g_attn[l]) @ w_in[l])
        oa, ob = prompt_mix(qa, ka, va, qi, ki, wi, qb, kb, vb, table_a, table_b, lam, lam_init, g_sub[l], topk_p)
        hp = hp + merge_heads(oa, ob) @ w_out[l]
        hp = hp + sq_relu_mlp(rmsnorm(hp, g_mlp[l]), w_up[l], w_down[l])
        rows_p.append((ka, va, ki, kb, vb))
        qa, ka, va, qi, ki, wi, qb, kb, vb = split_proj(rmsnorm(hs, g_attn[l]) @ w_in[l])
        oa, ob = sample_mix(l, qa, ka, va, qi, ki, wi, qb, kb, vb, cache_k_a, cache_v_a, cache_k_idx,
                            cache_k_b, cache_v_b, page_table, table_a, table_b, lam, lam_init, g_sub[l], topk_s)
        hs = hs + merge_heads(oa, ob) @ w_out[l]
        hs = hs + sq_relu_mlp(rmsnorm(hs, g_mlp[l]), w_up[l], w_down[l])
        rows_s.append((ka, va, ki, kb, vb))
    y_prompt = rmsnorm(hp, g_final)
    y_sample = rmsnorm(hs, g_final)
    k_a_p, v_a_p, kidx_p, k_b_p, v_b_p = [jnp.stack(r) for r in zip(*rows_p)]
    k_a_s, v_a_s, kidx_s, k_b_s, v_b_s = [jnp.stack(r) for r in zip(*rows_s)]
    return (y_prompt, y_sample, k_a_p, v_a_p, kidx_p, k_b_p, v_b_p, k_a_s, v_a_s, kidx_s, k_b_s, v_b_s)
```

```python
import functools
import math

import jax
import jax.numpy as jnp
import numpy as np
from jax import lax
from jax.experimental import pallas as pl
from jax.experimental.pallas import tpu as pltpu

HEAD_DIM = 128
D_HALF = HEAD_DIM // 2
IDX_HEADS = 16
IDX_DIM = 64
TOPK_MAX = 256
N_BUCKETS = 32
MAX_DISTANCE = 128
EPS = 1e-6
NEG = -1e30
INT_MIN = -(2 ** 31)

F32 = jnp.float32
MXU_DT = jnp.bfloat16

V7X_VMEM_BYTES = 64 * 1024 * 1024
VMEM_LIMIT = V7X_VMEM_BYTES - 8 * 1024 * 1024
LANES = 128

N_MAIN = 7
SMALL_W = 256


def _cparams(sem):
    return pltpu.CompilerParams(dimension_semantics=sem, vmem_limit_bytes=VMEM_LIMIT)


def _dot(a, b):
    return jnp.dot(a, b, preferred_element_type=F32)


def _dot_nt(a, b):
    return lax.dot_general(a, b, (((1,), (1,)), ((), ())), preferred_element_type=F32)


def _rms(x, g):
    return x * lax.rsqrt(jnp.mean(x * x, axis=-1, keepdims=True) + EPS) * g


def _inproj_kernel(x_ref, g_ref, w_ref, ws_ref,
                   qa_ref, qi_ref, qb_ref, ka_ref, va_ref, kb_ref, vb_ref,
                   kah_ref, vah_ref, kbh_ref, vbh_ref, sm_ref, xn_sc):
    j = pl.program_id(1)

    @pl.when(j == 0)
    def _():
        xn_sc[...] = _rms(x_ref[...], g_ref[...]).astype(xn_sc.dtype)
        sm_ref[...] = _dot(xn_sc[...], ws_ref[...])

    z = _dot(xn_sc[...], w_ref[...])
    q_out = {0: qa_ref, 3: qi_ref, 4: qb_ref}
    kv_out = {1: (ka_ref, kah_ref), 2: (va_ref, vah_ref), 5: (kb_ref, kbh_ref), 6: (vb_ref, vbh_ref)}
    for jj in range(N_MAIN):
        @pl.when(j == jj)
        def _(jj=jj):
            if jj in q_out:
                q_out[jj][...] = z.astype(q_out[jj].dtype)
            else:
                full, half = kv_out[jj]
                full[...] = z
                half[...] = z.astype(half.dtype)


def _inproj(x, g, w_main, w_small, layer, tm):
    M, D = x.shape
    W = HEAD_DIM * 8
    grid = (M // tm, N_MAIN)
    row = lambda i, j: (i, 0)
    o_lo = jax.ShapeDtypeStruct((M, W), MXU_DT)
    o_f32 = jax.ShapeDtypeStruct((M, W), F32)
    outs = (o_lo, o_lo, o_lo, o_f32, o_f32, o_f32, o_f32, o_lo, o_lo, o_lo, o_lo,
            jax.ShapeDtypeStruct((M, SMALL_W), F32))
    out_specs = tuple([pl.BlockSpec((tm, W), row)] * 11 + [pl.BlockSpec((tm, SMALL_W), row)])
    return pl.pallas_call(
        _inproj_kernel,
        grid=grid,
        in_specs=[
            pl.BlockSpec((tm, D), row),
            pl.BlockSpec((None, 1, D), lambda i, j: (layer, 0, 0)),
            pl.BlockSpec((None, D, W), lambda i, j: (layer, 0, j)),
            pl.BlockSpec((None, D, SMALL_W), lambda i, j: (layer, 0, 0)),
        ],
        out_specs=out_specs,
        out_shape=outs,
        scratch_shapes=[pltpu.VMEM((tm, D), MXU_DT)],
        compiler_params=_cparams(("parallel", "arbitrary")),
        name="inproj",
    )(x, g, w_main, w_small)


def _bias_kernel(tab_ref, bkt_ref, o_ref):
    h = pl.program_id(0)
    bkt = bkt_ref[...]
    acc = jnp.full(bkt.shape, NEG, F32)
    for b in range(N_BUCKETS):
        acc = jnp.where(bkt == b, tab_ref[b, h], acc)
    o_ref[0] = acc


def _bias_lookup(rel_bias, bucket):
    R, C = bucket.shape
    H = rel_bias.shape[1]
    return pl.pallas_call(
        _bias_kernel,
        grid=(H,),
        in_specs=[pl.BlockSpec(memory_space=pltpu.SMEM),
                  pl.BlockSpec((R, C), lambda h: (0, 0))],
        out_specs=pl.BlockSpec((1, R, C), lambda h: (h, 0, 0)),
        out_shape=jax.ShapeDtypeStruct((H, R, C), F32),
        compiler_params=_cparams(("arbitrary",)),
        name="bias_lookup",
    )(rel_bias, bucket)


def _t5_bucket(dist):
    dist = jnp.maximum(dist, 0)
    max_exact = N_BUCKETS // 2
    d = jnp.maximum(dist, 1).astype(F32)
    large = max_exact + (jnp.log(d / max_exact) / math.log(MAX_DISTANCE / max_exact)
                         * (N_BUCKETS - max_exact)).astype(jnp.int32)
    large = jnp.minimum(large, N_BUCKETS - 1)
    return jnp.where(dist < max_exact, dist, large)


def _sort_key(s):
    b = lax.bitcast_convert_type(s, jnp.int32)
    return jnp.where(b < 0, b ^ jnp.int32(0x7FFFFFFF), b)


def _kth_largest_key(count_ge, rows, k):
    kk = jnp.float32(k)
    zero = jnp.zeros((rows, 1), jnp.int32)
    t0 = jnp.where(count_ge(zero) >= kk, zero, jnp.full((rows, 1), INT_MIN, jnp.int32))

    def body(it, t):
        cand = t + lax.shift_left(jnp.int32(1), jnp.int32(30) - it)
        return jnp.where(count_ge(cand) >= kk, cand, t)

    return lax.fori_loop(0, 31, body, t0)


def _count(mask):
    return jnp.sum(jnp.where(mask, 1.0, 0.0), axis=1, keepdims=True)


def _tri_incl(n):
    r = lax.broadcasted_iota(jnp.int32, (n, n), 0)
    c = lax.broadcasted_iota(jnp.int32, (n, n), 1)
    return jnp.where(r <= c, 1.0, 0.0).astype(MXU_DT)


def _flash_head(q, k_ref, v_ref, col, n_kb, tk, scale, bias_fn, guard):
    R = q.shape[0]
    reps = R // tk

    def body(kb, carry):
        m, l, acc = carry
        ks = pl.multiple_of(kb * tk, tk)
        lg = _dot_nt(q, k_ref[pl.ds(ks, tk), col]) * scale
        lg = (lg.reshape(reps, tk, tk) + bias_fn(kb)[None]).reshape(R, tk)
        m_new = jnp.maximum(m, jnp.max(lg, axis=1, keepdims=True))
        p = jnp.exp(lg - m_new)
        if guard:
            p = jnp.where(lg > 0.5 * NEG, p, 0.0)
        alpha = jnp.exp(m - m_new)
        l = alpha * l + jnp.sum(p, axis=1, keepdims=True)
        acc = alpha * acc + _dot(p.astype(MXU_DT), v_ref[pl.ds(ks, tk), col])
        return m_new, l, acc

    init = (jnp.full((R, 1), NEG, F32), jnp.zeros((R, 1), F32), jnp.zeros((R, HEAD_DIM), F32))
    return lax.fori_loop(0, n_kb, body, init)


def _dsa_prompt_kernel(qa_ref, qi_ref, smq_ref, smk_ref, k_ref, v_ref, bt_ref, o_ref,
                       key_sc, sel_sc, *, tq, seq, topk, n_heads):
    i = pl.program_id(1)
    ki2 = smk_ref[:, 0:2 * IDX_DIM].astype(MXU_DT)
    wsc = smq_ref[:, 2 * IDX_DIM:2 * IDX_DIM + IDX_HEADS] * (IDX_HEADS ** -0.5 * IDX_DIM ** -0.5)
    lane = lax.broadcasted_iota(jnp.int32, (tq, 2 * IDX_DIM), 1)
    s = jnp.zeros((tq, seq), F32)
    for pr in range(IDX_HEADS // 2):
        qp = qi_ref[:, pr * 128:(pr + 1) * 128]
        zero = jnp.zeros_like(qp)
        d_lo = _dot_nt(jnp.where(lane < IDX_DIM, qp, zero), ki2)
        d_hi = _dot_nt(jnp.where(lane >= IDX_DIM, qp, zero), ki2)
        s = s + wsc[:, 2 * pr:2 * pr + 1] * jnp.maximum(d_lo, 0.0)
        s = s + wsc[:, 2 * pr + 1:2 * pr + 2] * jnp.maximum(d_hi, 0.0)
    def causal_mask(c0, width):
        row = i * tq + lax.broadcasted_iota(jnp.int32, (tq, width), 0)
        return c0 + lax.broadcasted_iota(jnp.int32, (tq, width), 1) <= row

    causal = causal_mask(0, seq)
    key_sc[...] = _sort_key(jnp.where(causal, s, NEG))

    thr = _kth_largest_key(lambda c: _count(key_sc[...] >= c), tq, topk)
    row1 = i * tq + lax.broadcasted_iota(jnp.int32, (tq, 1), 0)
    few = row1 < topk
    cnt_ge = _count(key_sc[...] >= thr)
    sel_sc[...] = jnp.where(causal & ((key_sc[...] >= thr) | few), 0.0, NEG)
    tied = jnp.max(jnp.where((cnt_ge > topk) & jnp.logical_not(few), 1.0, 0.0))

    @pl.when(tied > 0.0)
    def _():
        need = jnp.float32(topk) - _count(key_sc[...] > thr)
        tri = _tri_incl(tq)
        carry = jnp.zeros((tq, 1), F32)
        for c in range(seq // tq):
            kc = key_sc[:, c * tq:(c + 1) * tq]
            eq = kc == thr
            cs = _dot(jnp.where(eq, 1.0, 0.0).astype(MXU_DT), tri) + carry
            keep = (kc > thr) | few | (eq & (cs <= need))
            sel_sc[:, c * tq:(c + 1) * tq] = jnp.where(causal_mask(c * tq, tq) & keep, 0.0, NEG)
            carry = cs[:, tq - 1:tq]

    for h in range(n_heads):
        col = slice(h * HEAD_DIM, (h + 1) * HEAD_DIM)

        def bias_fn(kb, h=h):
            ks = pl.multiple_of(kb * tq, tq)
            return bt_ref[h, jnp.clip(kb - i + 2, 0, 2)] + sel_sc[:, pl.ds(ks, tq)]

        _, l, acc = _flash_head(qa_ref[:, col], k_ref, v_ref, col, i + 1, tq, HEAD_DIM ** -0.5, bias_fn, True)
        o_ref[:, col] = (acc / l).astype(o_ref.dtype)


def _diff_prompt_kernel(qb_ref, k_ref, v_ref, bt_ref, gsub_ref, lam_ref, o_ref,
                        *, tq, n_heads, lam_init):
    i = pl.program_id(1)
    lq = lam_ref[...]
    lam = (jnp.exp(jnp.sum(lq[0:1] * lq[1:2], axis=1, keepdims=True))
           - jnp.exp(jnp.sum(lq[2:3] * lq[3:4], axis=1, keepdims=True)) + lam_init)
    lane = lax.broadcasted_iota(jnp.int32, (tq, HEAD_DIM), 1)
    for h in range(n_heads):
        col = slice(h * HEAD_DIM, (h + 1) * HEAD_DIM)
        qh = qb_ref[:, col]
        zero = jnp.zeros_like(qh)
        q2 = jnp.concatenate([jnp.where(lane < D_HALF, qh, zero), jnp.where(lane >= D_HALF, qh, zero)], axis=0)

        def bias_fn(kb, h=h):
            return bt_ref[h, jnp.clip(kb - i + 2, 0, 2)]

        _, l, acc = _flash_head(q2, k_ref, v_ref, col, i + 1, tq, D_HALF ** -0.5, bias_fn, False)
        o = acc / l
        o = o[:tq] - lam * o[tq:]
        o = _rms(o, gsub_ref[...]) * (1.0 - lam_init)
        o_ref[:, col] = o.astype(o_ref.dtype)


def _prompt_attention(qa, qi, qb, kah, vah, kbh, vbh, small, bias_a, bias_b, g_sub_l, lam_vecs,
                      batch, seq, tq, topk, lam_init):
    M, W = qa.shape
    nq = seq // tq
    n_heads = W // HEAD_DIM
    grid = (batch, nq)
    qrow = lambda b, i: (b * nq + i, 0)
    brow = lambda b, i: (b, 0)
    const4 = lambda b, i: (0, 0, 0, 0)
    oa = pl.pallas_call(
        functools.partial(_dsa_prompt_kernel, tq=tq, seq=seq, topk=topk, n_heads=n_heads),
        grid=grid,
        in_specs=[
            pl.BlockSpec((tq, W), qrow),
            pl.BlockSpec((tq, W), qrow),
            pl.BlockSpec((tq, SMALL_W), qrow),
            pl.BlockSpec((seq, SMALL_W), brow),
            pl.BlockSpec((seq, W), brow),
            pl.BlockSpec((seq, W), brow),
            pl.BlockSpec((n_heads, 3, tq, tq), const4),
        ],
        out_specs=pl.BlockSpec((tq, W), qrow),
        out_shape=jax.ShapeDtypeStruct((M, W), MXU_DT),
        scratch_shapes=[pltpu.VMEM((tq, seq), jnp.int32), pltpu.VMEM((tq, seq), F32)],
        compiler_params=_cparams(("parallel", "arbitrary")),
        name="dsa_prompt",
    )(qa, qi, small, small, kah, vah, bias_a)
    ob = pl.pallas_call(
        functools.partial(_diff_prompt_kernel, tq=tq, n_heads=n_heads, lam_init=lam_init),
        grid=grid,
        in_specs=[
            pl.BlockSpec((tq, W), qrow),
            pl.BlockSpec((seq, W), brow),
            pl.BlockSpec((seq, W), brow),
            pl.BlockSpec((n_heads, 3, tq, tq), const4),
            pl.BlockSpec((1, HEAD_DIM), lambda b, i: (0, 0)),
            pl.BlockSpec((4, D_HALF), lambda b, i: (0, 0)),
        ],
        out_specs=pl.BlockSpec((tq, W), qrow),
        out_shape=jax.ShapeDtypeStruct((M, W), MXU_DT),
        compiler_params=_cparams(("parallel", "arbitrary")),
        name="diff_prompt",
    )(qb, kbh, vbh, bias_b, g_sub_l, lam_vecs)
    return oa, ob


def _outproj_kernel(oa_ref, ob_ref, w_ref, res_ref, o_ref):
    wa = oa_ref.shape[1]
    o_ref[...] = res_ref[...] + _dot(oa_ref[...], w_ref[0:wa, :]) + _dot(ob_ref[...], w_ref[wa:, :])


def _outproj(oa, ob, w_out, res, layer, tm):
    M, D = res.shape
    wa = oa.shape[1]
    row = lambda i: (i, 0)
    return pl.pallas_call(
        _outproj_kernel,
        grid=(M // tm,),
        in_specs=[pl.BlockSpec((tm, wa), row), pl.BlockSpec((tm, wa), row),
                  pl.BlockSpec((None, 2 * wa, D), lambda i: (layer, 0, 0)),
                  pl.BlockSpec((tm, D), row)],
        out_specs=pl.BlockSpec((tm, D), row),
        out_shape=jax.ShapeDtypeStruct((M, D), F32),
        compiler_params=_cparams(("parallel",)),
        name="outproj",
    )(oa, ob, w_out, res)


def _mlp_kernel(x_ref, g_ref, wu_ref, wd_ref, gf_ref, o_ref, xn_sc, acc_sc, *, final_norm):
    f = pl.program_id(1)

    @pl.when(f == 0)
    def _():
        xn_sc[...] = _rms(x_ref[...], g_ref[...]).astype(xn_sc.dtype)
        acc_sc[...] = jnp.zeros_like(acc_sc)

    h = jnp.maximum(_dot(xn_sc[...], wu_ref[...]), 0.0)
    acc_sc[...] += _dot((h * h).astype(MXU_DT), wd_ref[...])

    @pl.when(f == pl.num_programs(1) - 1)
    def _():
        y = x_ref[...] + acc_sc[...]
        o_ref[...] = _rms(y, gf_ref[...]) if final_norm else y


def _mlp(x, g_mlp, w_up, w_down, g_final, layer, tm, tf, final_norm):
    M, D = x.shape
    d_ff = w_up.shape[2]
    row = lambda i, f: (i, 0)
    return pl.pallas_call(
        functools.partial(_mlp_kernel, final_norm=final_norm),
        grid=(M // tm, d_ff // tf),
        in_specs=[pl.BlockSpec((tm, D), row),
                  pl.BlockSpec((None, 1, D), lambda i, f: (layer, 0, 0)),
                  pl.BlockSpec((None, D, tf), lambda i, f: (layer, 0, f)),
                  pl.BlockSpec((None, tf, D), lambda i, f: (layer, f, 0)),
                  pl.BlockSpec((1, D), lambda i, f: (0, 0))],
        out_specs=pl.BlockSpec((tm, D), row),
        out_shape=jax.ShapeDtypeStruct((M, D), F32),
        scratch_shapes=[pltpu.VMEM((tm, D), MXU_DT), pltpu.VMEM((tm, D), F32)],
        compiler_params=_cparams(("parallel", "arbitrary")),
        name="mlp",
    )(x, g_mlp, w_up, w_down, g_final)


def _idx_scores(qif, wcol, kpage, t_new):
    d = _dot_nt(qif, kpage.astype(MXU_DT))
    x = jnp.maximum(d, 0.0) * wcol
    return jnp.sum(x.reshape(t_new, IDX_HEADS, x.shape[1]), axis=1)


def _sample_idx_kernel(pt_ref, qif_ref, wcol_ref, *refs, pp, t_new):
    kid_refs, o_ref = refs[:pp], refs[pp]
    for j in range(pp):
        o_ref[0, :, j * LANES:(j + 1) * LANES] = _idx_scores(qif_ref[0], wcol_ref[0], kid_refs[j][0, 0], t_new)


def _sample_idx(page_table, qif, wcol, cache_k_idx, layer, pp, t_new):
    bd, n_pages = page_table.shape
    page = cache_k_idx.shape[2]
    nc = n_pages // pp
    per_b = lambda b, c, pt: (b, 0, 0)
    page_specs = [pl.BlockSpec((1, 1, page, IDX_DIM),
                               lambda b, c, pt, j=j: (layer, pt[b, c * pp + j], 0, 0)) for j in range(pp)]
    return pl.pallas_call(
        functools.partial(_sample_idx_kernel, pp=pp, t_new=t_new),
        grid_spec=pltpu.PrefetchScalarGridSpec(
            num_scalar_prefetch=1,
            grid=(bd, nc),
            in_specs=[pl.BlockSpec((1, IDX_HEADS * t_new, IDX_DIM), per_b),
                      pl.BlockSpec((1, IDX_HEADS * t_new, 1), per_b)] + page_specs,
            out_specs=pl.BlockSpec((1, t_new, pp * page), lambda b, c, pt: (b, 0, c)),
        ),
        out_shape=jax.ShapeDtypeStruct((bd, t_new, n_pages * page), F32),
        compiler_params=_cparams(("parallel", "arbitrary")),
        name="sample_idx",
    )(page_table, qif, wcol, *([cache_k_idx] * pp))


def _softmax_step(state, lg, v, guard):
    m, l, acc = state
    m_new = jnp.maximum(m, jnp.max(lg, axis=1, keepdims=True))
    p = jnp.exp(lg - m_new)
    if guard:
        p = jnp.where(lg > 0.5 * NEG, p, 0.0)
    alpha = jnp.exp(m - m_new)
    return (m_new, alpha * l + jnp.sum(p, axis=1, keepdims=True), alpha * acc + _dot(p.astype(MXU_DT), v))


def _sample_mix_kernel(pt_ref, sp_ref, qif_ref, wcol_ref, kin_ref, qa_ref, qb_ref,
                       ba_ref, bb_ref, gsub_ref, lam_ref, kan_ref, van_ref, kbn_ref, vbn_ref, *refs,
                       pp, t_new, topk, n_heads, lam_init):
    ka_refs, va_refs = refs[0:pp], refs[pp:2 * pp]
    kb_refs, vb_refs = refs[2 * pp:3 * pp], refs[3 * pp:4 * pp]
    oa_ref, ob_ref = refs[4 * pp], refs[4 * pp + 1]
    key_sc, selx_sc, ex_sc, ma_sc, la_sc, acca_sc, mb_sc, lb_sc, accb_sc = refs[4 * pp + 2:]
    c = pl.program_id(1)
    nc = pl.num_programs(1)
    past = sp_ref.shape[2]
    flat = n_heads * LANES
    ra = n_heads * t_new
    scale_a = HEAD_DIM ** -0.5
    scale_b = D_HALF ** -0.5

    def expand_rows(x):
        return jnp.broadcast_to(x[:, None, :], (t_new, n_heads, x.shape[1])).reshape(ra, x.shape[1])

    @pl.when(c == 0)
    def _():
        s_new = _idx_scores(qif_ref[0], wcol_ref[0], kin_ref[0], t_new)
        qi_ = lax.broadcasted_iota(jnp.int32, (t_new, LANES), 0)
        kj_ = lax.broadcasted_iota(jnp.int32, (t_new, LANES), 1)
        key_sc[:, 0:past] = _sort_key(sp_ref[0])
        key_sc[:, past:past + LANES] = _sort_key(jnp.where(kj_ <= qi_, s_new, NEG))
        thr = _kth_largest_key(lambda cd: _count(key_sc[...] >= cd), t_new, topk)
        cnt_ge = _count(key_sc[...] >= thr)
        selx_sc[...] = expand_rows(jnp.where(key_sc[...] >= thr, 1.0, 0.0)).astype(selx_sc.dtype)
        tied = jnp.max(jnp.where(cnt_ge > topk, 1.0, 0.0))

        @pl.when(tied > 0.0)
        def _():
            need = jnp.float32(topk) - _count(key_sc[...] > thr)
            tri = _tri_incl(LANES)

            def chunk(ci, carry):
                cs_ = pl.multiple_of(ci * LANES, LANES)
                kc = key_sc[:, pl.ds(cs_, LANES)]
                eq = kc == thr
                cs = _dot(jnp.where(eq, 1.0, 0.0).astype(MXU_DT), tri) + carry
                keep = (kc > thr) | (eq & (cs <= need))
                selx_sc[:, pl.ds(cs_, LANES)] = expand_rows(jnp.where(keep, 1.0, 0.0)).astype(selx_sc.dtype)
                return cs[:, LANES - 1:LANES]

            lax.fori_loop(0, past // LANES + 1, chunk, jnp.zeros((t_new, 1), F32))

        kk = lax.broadcasted_iota(jnp.int32, (LANES, flat), 0)
        ll = lax.broadcasted_iota(jnp.int32, (LANES, flat), 1)
        ex_sc[...] = jnp.where(ll // n_heads == kk, 1.0, 0.0).astype(ex_sc.dtype)
        ma_sc[...] = jnp.full_like(ma_sc, NEG)
        la_sc[...] = jnp.zeros_like(la_sc)
        acca_sc[...] = jnp.zeros_like(acca_sc)
        mb_sc[...] = jnp.full_like(mb_sc, NEG)
        lb_sc[...] = jnp.zeros_like(lb_sc)
        accb_sc[...] = jnp.zeros_like(accb_sc)

    qa = qa_ref[0]
    qb8 = qb_ref[0]
    lane = lax.broadcasted_iota(jnp.int32, (ra, HEAD_DIM), 1)
    zero = jnp.zeros_like(qb8)
    qb = jnp.concatenate([jnp.where(lane < D_HALF, qb8, zero), jnp.where(lane >= D_HALF, qb8, zero)], axis=0)

    def flat_page(ref4):
        x = ref4[0, 0]
        return x.reshape(x.shape[0] * x.shape[1], x.shape[2]).astype(MXU_DT)

    def one_page(st_a, st_b, kaf, vaf, kbf, vbf, key_off, tile):
        sel = _dot(selx_sc[:, pl.ds(key_off, LANES)], ex_sc[...])
        lga = _dot_nt(qa, kaf) * scale_a + ba_ref[tile]
        lga = jnp.where(sel > 0.5, lga, NEG)
        st_a = _softmax_step(st_a, lga, vaf, True)
        lgb = _dot_nt(qb, kbf) * scale_b + bb_ref[tile]
        st_b = _softmax_step(st_b, lgb, vbf, False)
        return st_a, st_b

    st_a = (ma_sc[...], la_sc[...], acca_sc[...])
    st_b = (mb_sc[...], lb_sc[...], accb_sc[...])
    for j in range(pp):
        pg = c * pp + j
        tile = jnp.where(pg == nc * pp - 1, 1, 0)
        st_a, st_b = one_page(st_a, st_b, flat_page(ka_refs[j]), flat_page(va_refs[j]),
                              flat_page(kb_refs[j]), flat_page(vb_refs[j]),
                              pl.multiple_of(pg * LANES, LANES), tile)
    ma_sc[...], la_sc[...], acca_sc[...] = st_a
    mb_sc[...], lb_sc[...], accb_sc[...] = st_b

    @pl.when(c == nc - 1)
    def _():
        def new_page(ref3):
            return ref3[0].astype(MXU_DT)
        fa, fb = one_page((ma_sc[...], la_sc[...], acca_sc[...]), (mb_sc[...], lb_sc[...], accb_sc[...]),
                          new_page(kan_ref), new_page(van_ref), new_page(kbn_ref), new_page(vbn_ref), past, 2)
        oa_ref[0] = (fa[2] / fa[1]).astype(oa_ref.dtype)
        lq = lam_ref[...]
        lam = (jnp.exp(jnp.sum(lq[0:1] * lq[1:2], axis=1, keepdims=True))
               - jnp.exp(jnp.sum(lq[2:3] * lq[3:4], axis=1, keepdims=True)) + lam_init)
        o = fb[2] / fb[1]
        o = o[:ra] - lam * o[ra:]
        ob_ref[0] = (_rms(o, gsub_ref[...]) * (1.0 - lam_init)).astype(ob_ref.dtype)


def _sample_mix(page_table, s_past, qif, wcol, ki_new, qa8, qb8, bias_a, bias_b, g_sub_l, lam_vecs,
                new_pages, caches, layer, pp, t_new, topk, lam_init):
    bd, n_pages = page_table.shape
    cache_k_a = caches[0]
    page, n_heads = cache_k_a.shape[2], cache_k_a.shape[3]
    past = n_pages * page
    nc = n_pages // pp
    ra = n_heads * t_new
    flat = n_heads * LANES
    per_b = lambda b, c, pt: (b, 0, 0)
    const3 = lambda b, c, pt: (0, 0, 0)
    const2 = lambda b, c, pt: (0, 0)

    def page_specs():
        return [pl.BlockSpec((1, 1, page, n_heads, HEAD_DIM),
                             lambda b, c, pt, j=j: (layer, pt[b, c * pp + j], 0, 0, 0)) for j in range(pp)]

    in_specs = [
        pl.BlockSpec((1, t_new, past), per_b),
        pl.BlockSpec((1, IDX_HEADS * t_new, IDX_DIM), per_b),
        pl.BlockSpec((1, IDX_HEADS * t_new, 1), per_b),
        pl.BlockSpec((1, LANES, IDX_DIM), per_b),
        pl.BlockSpec((1, ra, HEAD_DIM), per_b),
        pl.BlockSpec((1, ra, HEAD_DIM), per_b),
        pl.BlockSpec((3, ra, flat), const3),
        pl.BlockSpec((3, 2 * ra, flat), const3),
        pl.BlockSpec((1, HEAD_DIM), const2),
        pl.BlockSpec((4, D_HALF), const2),
    ] + [pl.BlockSpec((1, flat, HEAD_DIM), per_b)] * 4 + page_specs() + page_specs() + page_specs() + page_specs()
    out_spec = pl.BlockSpec((1, ra, HEAD_DIM), per_b)
    scratch = [
        pltpu.VMEM((t_new, past + LANES), jnp.int32),
        pltpu.VMEM((ra, past + LANES), MXU_DT),
        pltpu.VMEM((LANES, flat), MXU_DT),
        pltpu.VMEM((ra, 1), F32), pltpu.VMEM((ra, 1), F32), pltpu.VMEM((ra, HEAD_DIM), F32),
        pltpu.VMEM((2 * ra, 1), F32), pltpu.VMEM((2 * ra, 1), F32), pltpu.VMEM((2 * ra, HEAD_DIM), F32),
    ]
    ck_a, cv_a, ck_b, cv_b = caches
    return pl.pallas_call(
        functools.partial(_sample_mix_kernel, pp=pp, t_new=t_new, topk=topk, n_heads=n_heads,
                          lam_init=lam_init),
        grid_spec=pltpu.PrefetchScalarGridSpec(
            num_scalar_prefetch=1, grid=(bd, nc), in_specs=in_specs,
            out_specs=(out_spec, out_spec), scratch_shapes=scratch),
        out_shape=(jax.ShapeDtypeStruct((bd, ra, HEAD_DIM), MXU_DT),) * 2,
        compiler_params=_cparams(("parallel", "arbitrary")),
        name="sample_mix",
    )(page_table, s_past, qif, wcol, ki_new, qa8, qb8, bias_a, bias_b, g_sub_l, lam_vecs,
      *new_pages, *([ck_a] * pp), *([cv_a] * pp), *([ck_b] * pp), *([cv_b] * pp))


def _largest_divisor(n, cap):
    d = min(n, cap)
    while n % d:
        d -= 1
    return d


def _tiles(m_prompt, seq, n_pages, d_ff):
    return dict(
        tm_in=_largest_divisor(m_prompt, 256),
        tm_out=_largest_divisor(m_prompt, 512),
        tm_mlp=_largest_divisor(m_prompt, 512),
        tf=_largest_divisor(d_ff, 1024),
        tq=_largest_divisor(seq, 256),
        pp=_largest_divisor(n_pages, 8),
    )


def kernel(x_prompt, x_sample, cache_k_a, cache_v_a, cache_k_idx, cache_k_b, cache_v_b, page_table, w_in, w_out, w_up, w_down, g_attn, g_mlp, g_sub, lam_q1, lam_k1, lam_q2, lam_k2, rel_bias, g_final):
    batch, seq, d_model = x_prompt.shape
    bd, t_new, _ = x_sample.shape
    depth = w_in.shape[0]
    n_pool, page, h_a = cache_k_a.shape[1], cache_k_a.shape[2], cache_k_a.shape[3]
    h_b = cache_k_b.shape[3]
    n_pages = page_table.shape[1]
    past = n_pages * page
    wa, wb = h_a * HEAD_DIM, h_b * HEAD_DIM
    d_ff = w_up.shape[2]
    assert h_a == h_b and wa == 1024 and page == LANES and d_model == wa + wb
    topk_p = min(TOPK_MAX, seq // 4)
    topk_s = min(TOPK_MAX, (past + t_new) // 4)
    assert topk_s <= past and t_new <= LANES
    mp, ms = batch * seq, bd * t_new
    t = _tiles(mp, seq, n_pages, d_ff)
    tq, pp = t["tq"], t["pp"]

    sizes = (wa, wa, wa, IDX_HEADS * IDX_DIM, IDX_DIM, IDX_HEADS, wb, wb, wb)
    offs = np.concatenate([[0], np.cumsum(sizes)])
    seg = lambda n: w_in[:, :, int(offs[n]):int(offs[n + 1])]
    w_main = jnp.concatenate([seg(0), seg(1), seg(2), seg(3), seg(6), seg(7), seg(8)], axis=-1).astype(MXU_DT)
    pad = jnp.zeros(w_in.shape[:2] + (SMALL_W - 2 * IDX_DIM - IDX_HEADS,), w_in.dtype)
    w_small = jnp.concatenate([seg(4), seg(4), seg(5), pad], axis=-1).astype(MXU_DT)
    w_out_c, w_up_c, w_down_c = w_out.astype(MXU_DT), w_up.astype(MXU_DT), w_down.astype(MXU_DT)
    lam_all = jnp.stack([lam_q1, lam_k1, lam_q2, lam_k2], axis=1)
    g_attn, g_mlp = g_attn.reshape(depth, 1, d_model), g_mlp.reshape(depth, 1, d_model)
    g_final2 = g_final.reshape(1, d_model)

    r = jnp.arange(tq, dtype=jnp.int32)[:, None]
    cc = jnp.arange(tq, dtype=jnp.int32)[None, :]
    d_prev, d_diag = r + tq - cc, r - cc
    assert tq >= MAX_DISTANCE
    bkt_p = jnp.concatenate([jnp.full((tq, tq), N_BUCKETS - 1, jnp.int32),
                             _t5_bucket(d_prev),
                             jnp.where(d_diag >= 0, _t5_bucket(d_diag), -1)], axis=0)
    bias_p = _bias_lookup(rel_bias, bkt_p).reshape(h_a + h_b, 3, tq, tq)
    bias_pa, bias_pb = bias_p[:h_a], bias_p[h_a:]

    qpos = past + jnp.arange(t_new, dtype=jnp.int32)[:, None]
    kl = jnp.arange(LANES, dtype=jnp.int32)[None, :]
    d_last = qpos - (past - LANES + kl)
    d_new = qpos - (past + kl)
    bkt_s = jnp.concatenate([jnp.full((t_new, LANES), N_BUCKETS - 1, jnp.int32),
                             _t5_bucket(d_last),
                             jnp.where((d_new >= 0) & (kl < t_new), _t5_bucket(d_new), -1)], axis=0)
    assert past - LANES >= 0 and t_new + LANES >= MAX_DISTANCE
    bias_s = _bias_lookup(rel_bias, bkt_s).reshape(h_a + h_b, 3, t_new, LANES)

    def flat_tiles(bs, reps):
        hh = bs.shape[0]
        x = jnp.transpose(bs, (1, 2, 0, 3))
        eye = jnp.eye(hh, dtype=bool)[None, None, :, None, :]
        x = jnp.where(eye, x[..., None], NEG).reshape(3, t_new * hh, LANES * hh)
        return jnp.concatenate([x] * reps, axis=1)

    bias_sa, bias_sb = flat_tiles(bias_s[:h_a], 1), flat_tiles(bias_s[h_a:], 2)

    hp = x_prompt.reshape(mp, d_model)
    hs = x_sample.reshape(ms, d_model)
    rows_p, rows_s = [], []
    for l in range(depth):
        lam_init = 0.8 - 0.6 * math.exp(-0.3 * l)
        last = l == depth - 1
        qa, qi, qb, ka, va, kb, vb, kah, vah, kbh, vbh, small = _inproj(hp, g_attn, w_main, w_small, l, t["tm_in"])
        oa, ob = _prompt_attention(qa, qi, qb, kah, vah, kbh, vbh, small, bias_pa, bias_pb,
                                   g_sub[l:l + 1], lam_all[l], batch, seq, tq, topk_p, lam_init)
        hp = _outproj(oa, ob, w_out_c, hp, l, t["tm_out"])
        hp = _mlp(hp, g_mlp, w_up_c, w_down_c, g_final2, l, t["tm_mlp"], t["tf"], last)
        rows_p.append((ka, va, small[:, :IDX_DIM], kb, vb))
        qa, qi, qb, ka, va, kb, vb, _, _, _, _, small = _inproj(hs, g_attn, w_main, w_small, l, ms)
        ki = small[:, :IDX_DIM]
        qif = qi.reshape(bd, t_new * IDX_HEADS, IDX_DIM)
        wcol = (small[:, 2 * IDX_DIM:2 * IDX_DIM + IDX_HEADS]
                * (IDX_HEADS ** -0.5 * IDX_DIM ** -0.5)).reshape(bd, t_new * IDX_HEADS, 1)
        s_past = _sample_idx(page_table, qif, wcol, cache_k_idx, l, pp, t_new)

        def as_page(x, width):
            x = x.reshape(bd, t_new, width // HEAD_DIM, HEAD_DIM)
            x = jnp.pad(x, ((0, 0), (0, LANES - t_new), (0, 0), (0, 0)))
            return x.reshape(bd, LANES * (width // HEAD_DIM), HEAD_DIM)

        ki_new = jnp.pad(ki.reshape(bd, t_new, IDX_DIM), ((0, 0), (0, LANES - t_new), (0, 0)))
        oa, ob = _sample_mix(page_table, s_past, qif, wcol, ki_new,
                             qa.reshape(bd, t_new * h_a, HEAD_DIM), qb.reshape(bd, t_new * h_b, HEAD_DIM),
                             bias_sa, bias_sb, g_sub[l:l + 1], lam_all[l],
                             (as_page(ka, wa), as_page(va, wa), as_page(kb, wb), as_page(vb, wb)),
                             (cache_k_a, cache_v_a, cache_k_b, cache_v_b), l, pp, t_new, topk_s, lam_init)
        hs = _outproj(oa.reshape(ms, wa), ob.reshape(ms, wb), w_out_c, hs, l, ms)
        hs = _mlp(hs, g_mlp, w_up_c, w_down_c, g_final2, l, ms, t["tf"], last)
        rows_s.append((ka, va, ki, kb, vb))

    def stack(rows, n, lead):
        outs = []
        for idx in range(5):
            a = jnp.stack([r_[idx] for r_ in rows])
            outs.append(a.reshape((depth,) + lead + ((IDX_DIM,) if idx == 2 else (n, HEAD_DIM))))
        return outs

    k_a_p, v_a_p, kidx_p, k_b_p, v_b_p = stack(rows_p, h_a, (batch, seq))
    k_a_s, v_a_s, kidx_s, k_b_s, v_b_s = stack(rows_s, h_a, (bd, t_new))
    y_prompt = hp.reshape(batch, seq, d_model)
    y_sample = hs.reshape(bd, t_new, d_model)
    return (y_prompt, y_sample, k_a_p, v_a_p, kidx_p, k_b_p, v_b_p, k_a_s, v_a_s, kidx_s, k_b_s, v_b_s)
```

```python
import functools
import math

import jax
import jax.numpy as jnp
import numpy as np
from jax import lax
from jax.experimental import pallas as pl
from jax.experimental.pallas import tpu as pltpu

HEAD_DIM = 128
D_HALF = HEAD_DIM // 2
IDX_HEADS = 16
IDX_DIM = 64
TOPK_MAX = 256
N_BUCKETS = 32
MAX_DISTANCE = 128
EPS = 1e-6
NEG = -1e30
INT_MIN = -(2 ** 31)

F32 = jnp.float32
MXU_DT = jnp.bfloat16

V7X_VMEM_BYTES = 64 * 1024 * 1024
VMEM_LIMIT = V7X_VMEM_BYTES - 8 * 1024 * 1024
LANES = 128

N_MAIN = 7
SMALL_W = 256
IDX_W_SCALE = IDX_HEADS ** -0.5 * IDX_DIM ** -0.5


def _cparams(sem):
    return pltpu.CompilerParams(dimension_semantics=sem, vmem_limit_bytes=VMEM_LIMIT)


def _dot(a, b):
    return jnp.dot(a, b, preferred_element_type=F32)


def _dot_nt(a, b):
    return lax.dot_general(a, b, (((1,), (1,)), ((), ())), preferred_element_type=F32)


def _rms(x, g):
    return x * lax.rsqrt(jnp.mean(x * x, axis=-1, keepdims=True) + EPS) * g


def _lambda(lam_ref, lam_init):
    lq = lam_ref[...]
    return (jnp.exp(jnp.sum(lq[0:1] * lq[1:2], axis=1, keepdims=True))
            - jnp.exp(jnp.sum(lq[2:3] * lq[3:4], axis=1, keepdims=True)) + lam_init)


def _inproj_kernel(x_ref, g_ref, w_ref, ws_ref,
                   qa_ref, qi_ref, qb_ref, ka_ref, va_ref, kb_ref, vb_ref,
                   kah_ref, vah_ref, kbh_ref, vbh_ref, sm_ref, xn_sc):
    j = pl.program_id(1)

    @pl.when(j == 0)
    def _():
        xn_sc[...] = _rms(x_ref[...], g_ref[...]).astype(xn_sc.dtype)
        sm_ref[...] = _dot(xn_sc[...], ws_ref[...])

    z = _dot(xn_sc[...], w_ref[...])
    q_out = {0: qa_ref, 3: qi_ref, 4: qb_ref}
    kv_out = {1: (ka_ref, kah_ref), 2: (va_ref, vah_ref), 5: (kb_ref, kbh_ref), 6: (vb_ref, vbh_ref)}
    for jj in range(N_MAIN):
        @pl.when(j == jj)
        def _(jj=jj):
            if jj in q_out:
                q_out[jj][...] = z.astype(q_out[jj].dtype)
            else:
                full, half = kv_out[jj]
                full[...] = z
                half[...] = z.astype(half.dtype)


def _inproj(x, g, w_main, w_small, layer, tm):
    M, D = x.shape
    W = HEAD_DIM * 8
    grid = (M // tm, N_MAIN)
    row = lambda i, j: (i, 0)
    o_lo = jax.ShapeDtypeStruct((M, W), MXU_DT)
    o_f32 = jax.ShapeDtypeStruct((M, W), F32)
    outs = (o_lo, o_lo, o_lo, o_f32, o_f32, o_f32, o_f32, o_lo, o_lo, o_lo, o_lo,
            jax.ShapeDtypeStruct((M, SMALL_W), F32))
    out_specs = tuple([pl.BlockSpec((tm, W), row)] * 11 + [pl.BlockSpec((tm, SMALL_W), row)])
    return pl.pallas_call(
        _inproj_kernel,
        grid=grid,
        in_specs=[
            pl.BlockSpec((tm, D), row),
            pl.BlockSpec((None, 1, D), lambda i, j: (layer, 0, 0)),
            pl.BlockSpec((None, D, W), lambda i, j: (layer, 0, j)),
            pl.BlockSpec((None, D, SMALL_W), lambda i, j: (layer, 0, 0)),
        ],
        out_specs=out_specs,
        out_shape=outs,
        scratch_shapes=[pltpu.VMEM((tm, D), MXU_DT)],
        compiler_params=_cparams(("parallel", "arbitrary")),
        name="inproj",
    )(x, g, w_main, w_small)


def _bias_kernel(tab_ref, bkt_ref, o_ref):
    h = pl.program_id(0)
    bkt = bkt_ref[...]
    acc = jnp.full(bkt.shape, NEG, F32)
    for b in range(N_BUCKETS):
        acc = jnp.where(bkt == b, tab_ref[b, h], acc)
    o_ref[0] = acc


def _bias_lookup(rel_bias, bucket):
    R, C = bucket.shape
    H = rel_bias.shape[1]
    return pl.pallas_call(
        _bias_kernel,
        grid=(H,),
        in_specs=[pl.BlockSpec(memory_space=pltpu.SMEM),
                  pl.BlockSpec((R, C), lambda h: (0, 0))],
        out_specs=pl.BlockSpec((1, R, C), lambda h: (h, 0, 0)),
        out_shape=jax.ShapeDtypeStruct((H, R, C), F32),
        compiler_params=_cparams(("arbitrary",)),
        name="bias_lookup",
    )(rel_bias, bucket)


def _t5_bucket(dist):
    dist = jnp.maximum(dist, 0)
    max_exact = N_BUCKETS // 2
    d = jnp.maximum(dist, 1).astype(F32)
    large = max_exact + (jnp.log(d / max_exact) / math.log(MAX_DISTANCE / max_exact)
                         * (N_BUCKETS - max_exact)).astype(jnp.int32)
    large = jnp.minimum(large, N_BUCKETS - 1)
    return jnp.where(dist < max_exact, dist, large)


def _sort_key(s):
    b = lax.bitcast_convert_type(s, jnp.int32)
    return jnp.where(b < 0, b ^ jnp.int32(0x7FFFFFFF), b)


def _kth_largest_key(count_ge, shape, k):
    kk = jnp.float32(k)
    zero = jnp.zeros(shape, jnp.int32)
    t0 = jnp.where(count_ge(zero) >= kk, zero, jnp.full(shape, INT_MIN, jnp.int32))

    def body(it, t):
        cand = t + lax.shift_left(jnp.int32(1), jnp.int32(30) - it)
        return jnp.where(count_ge(cand) >= kk, cand, t)

    return lax.fori_loop(0, 31, body, t0)


def _count(mask, axis):
    return jnp.sum(jnp.where(mask, 1.0, 0.0), axis=axis, keepdims=True)


def _tri(n, lower):
    r = lax.broadcasted_iota(jnp.int32, (n, n), 0)
    c = lax.broadcasted_iota(jnp.int32, (n, n), 1)
    return jnp.where((r >= c) if lower else (r <= c), 1.0, 0.0).astype(MXU_DT)


def _head_cols(h):
    return slice(h * HEAD_DIM, (h + 1) * HEAD_DIM)


def _flash_heads_t(q_ts, heads, k_ref, vt_ref, n_kb, tk, tq, scale, bias_fn, guard, acc_sc):
    R = q_ts[0].shape[1]
    reps = R // tq
    for h in heads:
        acc_sc[h] = jnp.zeros((HEAD_DIM, R), F32)

    def body(kb, carry):
        ks = pl.multiple_of(kb * tk, tk)
        out = []
        for g, h in enumerate(heads):
            m, l = carry[g]
            hs = _head_cols(h)
            lg = _dot(k_ref[pl.ds(ks, tk), hs], q_ts[g]) * scale
            b = bias_fn(kb, h)
            lg = jnp.concatenate([lg[:, r * tq:(r + 1) * tq] + b for r in range(reps)], axis=1)
            m_new = jnp.maximum(m, jnp.max(lg, axis=0, keepdims=True))
            p = jnp.exp(lg - m_new)
            if guard:
                p = jnp.where(lg > 0.5 * NEG, p, 0.0)
            alpha = jnp.exp(m - m_new)
            l = alpha * l + jnp.sum(p, axis=0, keepdims=True)
            acc_sc[h] = alpha * acc_sc[h] + _dot(vt_ref[hs, pl.ds(ks, tk)], p.astype(MXU_DT))
            out.append((m_new, l))
        return tuple(out)

    init = tuple((jnp.full((1, R), NEG, F32), jnp.zeros((1, R), F32)) for _ in heads)
    return [ml[1] for ml in lax.fori_loop(0, n_kb, body, init)]


def _dsa_prompt_kernel(qat_ref, qit_ref, smt_ref, smk_ref, k_ref, vt_ref, bt_ref, o_ref,
                       key_sc, sel_sc, acc_sc, *, tq, topk, n_heads, heads_per_loop):
    i = pl.program_id(1)
    tk = tq
    n_kb = i + 1
    w_t = smt_ref[2 * IDX_DIM:2 * IDX_DIM + IDX_HEADS, :] * IDX_W_SCALE
    sub = lax.broadcasted_iota(jnp.int32, (2 * IDX_DIM, tq), 0)
    qpos = i * tq + lax.broadcasted_iota(jnp.int32, (tk, tq), 1)
    krel = lax.broadcasted_iota(jnp.int32, (tk, tq), 0)
    few = i * tq + lax.broadcasted_iota(jnp.int32, (1, tq), 1) < topk

    def causal(kb):
        return kb * tk + krel <= qpos

    def tile(kb):
        return pl.ds(pl.multiple_of(kb * tk, tk), tk)

    def score_tile(kb, carry):
        ki2 = smk_ref[tile(kb), 0:2 * IDX_DIM].astype(MXU_DT)
        s = jnp.zeros((tk, tq), F32)
        for pr in range(IDX_HEADS // 2):
            qp = qit_ref[pr * 128:(pr + 1) * 128, :]
            zero = jnp.zeros_like(qp)
            d_lo = _dot(ki2, jnp.where(sub < IDX_DIM, qp, zero))
            d_hi = _dot(ki2, jnp.where(sub >= IDX_DIM, qp, zero))
            s = s + w_t[2 * pr:2 * pr + 1, :] * jnp.maximum(d_lo, 0.0)
            s = s + w_t[2 * pr + 1:2 * pr + 2, :] * jnp.maximum(d_hi, 0.0)
        key_sc[tile(kb), :] = _sort_key(jnp.where(causal(kb), s, NEG))
        return carry

    lax.fori_loop(0, n_kb, score_tile, 0)

    def count_where(pred):
        def inner(kb, c):
            return c + _count(pred(key_sc[tile(kb), :]), 0)
        return lax.fori_loop(0, n_kb, inner, jnp.zeros((1, tq), F32))

    thr = _kth_largest_key(lambda cd: count_where(lambda kt: kt >= cd), (1, tq), topk)
    cnt_ge = count_where(lambda kt: kt >= thr)

    def sel_tile(kb, carry):
        keep = (key_sc[tile(kb), :] >= thr) | few
        sel_sc[tile(kb), :] = jnp.where(causal(kb) & keep, 0.0, NEG)
        return carry

    lax.fori_loop(0, n_kb, sel_tile, 0)
    tied = jnp.max(jnp.where((cnt_ge > topk) & jnp.logical_not(few), 1.0, 0.0))

    @pl.when(tied > 0.0)
    def _():
        need = jnp.float32(topk) - count_where(lambda kt: kt > thr)
        tri = _tri(tk, True)

        def tie_tile(kb, carry):
            kt = key_sc[tile(kb), :]
            eq = kt == thr
            cs = _dot(tri, jnp.where(eq, 1.0, 0.0).astype(MXU_DT)) + carry
            keep = (kt > thr) | few | (eq & (cs <= need))
            sel_sc[tile(kb), :] = jnp.where(causal(kb) & keep, 0.0, NEG)
            return cs[tk - 1:tk, :]

        lax.fori_loop(0, n_kb, tie_tile, jnp.zeros((1, tq), F32))

    def bias_fn(kb, h):
        return bt_ref[h, jnp.clip(kb - i + 2, 0, 2)] + sel_sc[tile(kb), :]

    for h0 in range(0, n_heads, heads_per_loop):
        heads = list(range(h0, h0 + heads_per_loop))
        ls = _flash_heads_t([qat_ref[_head_cols(h), :] for h in heads], heads, k_ref, vt_ref, n_kb, tk, tq,
                            HEAD_DIM ** -0.5, bias_fn, True, acc_sc)
        for h, l in zip(heads, ls):
            o_ref[:, _head_cols(h)] = jnp.transpose(acc_sc[h] / l).astype(o_ref.dtype)


def _diff_prompt_kernel(qbt_ref, k_ref, vt_ref, bt_ref, gsub_ref, lam_ref, o_ref, acc_sc,
                        *, tq, n_heads, lam_init, heads_per_loop):
    i = pl.program_id(1)
    lam = _lambda(lam_ref, lam_init)
    sub = lax.broadcasted_iota(jnp.int32, (HEAD_DIM, tq), 0)

    def two_maps(h):
        qh = qbt_ref[_head_cols(h), :]
        zero = jnp.zeros_like(qh)
        return jnp.concatenate([jnp.where(sub < D_HALF, qh, zero), jnp.where(sub >= D_HALF, qh, zero)], axis=1)

    def bias_fn(kb, h):
        return bt_ref[h, jnp.clip(kb - i + 2, 0, 2)]

    for h0 in range(0, n_heads, heads_per_loop):
        heads = list(range(h0, h0 + heads_per_loop))
        ls = _flash_heads_t([two_maps(h) for h in heads], heads, k_ref, vt_ref, i + 1, tq, tq,
                            D_HALF ** -0.5, bias_fn, False, acc_sc)
        for h, l in zip(heads, ls):
            o = acc_sc[h] / l
            o = o[:, :tq] - lam * o[:, tq:]
            o = o * lax.rsqrt(jnp.mean(o * o, axis=0, keepdims=True) + EPS) * gsub_ref[...]
            o_ref[:, _head_cols(h)] = jnp.transpose(o * (1.0 - lam_init)).astype(o_ref.dtype)


def _prompt_attention(qa_t, qi_t, qb_t, kah, vah_t, kbh, vbh_t, small, small_t, bias_a, bias_b,
                      g_sub_col, lam_vecs, batch, seq, tq, topk, lam_init, hpl_a, hpl_b):
    W, M = qa_t.shape
    nq = seq // tq
    n_heads = W // HEAD_DIM
    grid = (batch, nq)
    qrow = lambda b, i: (b * nq + i, 0)
    qcol = lambda b, i: (0, b * nq + i)
    brow = lambda b, i: (b, 0)
    bcol = lambda b, i: (0, b)
    const4 = lambda b, i: (0, 0, 0, 0)
    oa = pl.pallas_call(
        functools.partial(_dsa_prompt_kernel, tq=tq, topk=topk, n_heads=n_heads, heads_per_loop=hpl_a),
        grid=grid,
        in_specs=[
            pl.BlockSpec((W, tq), qcol),
            pl.BlockSpec((W, tq), qcol),
            pl.BlockSpec((SMALL_W, tq), qcol),
            pl.BlockSpec((seq, SMALL_W), brow),
            pl.BlockSpec((seq, W), brow),
            pl.BlockSpec((W, seq), bcol),
            pl.BlockSpec((n_heads, 3, tq, tq), const4),
        ],
        out_specs=pl.BlockSpec((tq, W), qrow),
        out_shape=jax.ShapeDtypeStruct((M, W), MXU_DT),
        scratch_shapes=[pltpu.VMEM((seq, tq), jnp.int32), pltpu.VMEM((seq, tq), F32),
                        pltpu.VMEM((n_heads, HEAD_DIM, tq), F32)],
        compiler_params=_cparams(("parallel", "arbitrary")),
        name="dsa_prompt",
    )(qa_t, qi_t, small_t, small, kah, vah_t, bias_a)
    ob = pl.pallas_call(
        functools.partial(_diff_prompt_kernel, tq=tq, n_heads=n_heads, lam_init=lam_init,
                          heads_per_loop=hpl_b),
        grid=grid,
        in_specs=[
            pl.BlockSpec((W, tq), qcol),
            pl.BlockSpec((seq, W), brow),
            pl.BlockSpec((W, seq), bcol),
            pl.BlockSpec((n_heads, 3, tq, tq), const4),
            pl.BlockSpec((HEAD_DIM, 1), lambda b, i: (0, 0)),
            pl.BlockSpec((4, D_HALF), lambda b, i: (0, 0)),
        ],
        out_specs=pl.BlockSpec((tq, W), qrow),
        out_shape=jax.ShapeDtypeStruct((M, W), MXU_DT),
        scratch_shapes=[pltpu.VMEM((n_heads, HEAD_DIM, 2 * tq), F32)],
        compiler_params=_cparams(("parallel", "arbitrary")),
        name="diff_prompt",
    )(qb_t, kbh, vbh_t, bias_b, g_sub_col, lam_vecs)
    return oa, ob


def _outproj_kernel(oa_ref, ob_ref, w_ref, res_ref, o_ref):
    wa = oa_ref.shape[1]
    o_ref[...] = res_ref[...] + _dot(oa_ref[...], w_ref[0:wa, :]) + _dot(ob_ref[...], w_ref[wa:, :])


def _outproj(oa, ob, w_out, res, layer, tm):
    M, D = res.shape
    wa = oa.shape[1]
    row = lambda i: (i, 0)
    return pl.pallas_call(
        _outproj_kernel,
        grid=(M // tm,),
        in_specs=[pl.BlockSpec((tm, wa), row), pl.BlockSpec((tm, wa), row),
                  pl.BlockSpec((None, 2 * wa, D), lambda i: (layer, 0, 0)),
                  pl.BlockSpec((tm, D), row)],
        out_specs=pl.BlockSpec((tm, D), row),
        out_shape=jax.ShapeDtypeStruct((M, D), F32),
        compiler_params=_cparams(("parallel",)),
        name="outproj",
    )(oa, ob, w_out, res)


def _mlp_kernel(x_ref, g_ref, wu_ref, wd_ref, gf_ref, o_ref, xn_sc, acc_sc, *, final_norm):
    f = pl.program_id(1)

    @pl.when(f == 0)
    def _():
        xn_sc[...] = _rms(x_ref[...], g_ref[...]).astype(xn_sc.dtype)
        acc_sc[...] = jnp.zeros_like(acc_sc)

    h = jnp.maximum(_dot(xn_sc[...], wu_ref[...]), 0.0)
    acc_sc[...] += _dot((h * h).astype(MXU_DT), wd_ref[...])

    @pl.when(f == pl.num_programs(1) - 1)
    def _():
        y = x_ref[...] + acc_sc[...]
        o_ref[...] = _rms(y, gf_ref[...]) if final_norm else y


def _mlp(x, g_mlp, w_up, w_down, g_final, layer, tm, tf, final_norm):
    M, D = x.shape
    d_ff = w_up.shape[2]
    row = lambda i, f: (i, 0)
    return pl.pallas_call(
        functools.partial(_mlp_kernel, final_norm=final_norm),
        grid=(M // tm, d_ff // tf),
        in_specs=[pl.BlockSpec((tm, D), row),
                  pl.BlockSpec((None, 1, D), lambda i, f: (layer, 0, 0)),
                  pl.BlockSpec((None, D, tf), lambda i, f: (layer, 0, f)),
                  pl.BlockSpec((None, tf, D), lambda i, f: (layer, f, 0)),
                  pl.BlockSpec((1, D), lambda i, f: (0, 0))],
        out_specs=pl.BlockSpec((tm, D), row),
        out_shape=jax.ShapeDtypeStruct((M, D), F32),
        scratch_shapes=[pltpu.VMEM((tm, D), MXU_DT), pltpu.VMEM((tm, D), F32)],
        compiler_params=_cparams(("parallel", "arbitrary")),
        name="mlp",
    )(x, g_mlp, w_up, w_down, g_final)


def _idx_scores(qif, wcol, kpage_t, t_new):
    d = _dot(qif, kpage_t.astype(MXU_DT))
    x = jnp.maximum(d, 0.0) * wcol
    return jnp.sum(x.reshape(t_new, IDX_HEADS, x.shape[1]), axis=1)


def _sample_idx_kernel(pt_ref, qif_ref, wcol_ref, *refs, pp, t_new):
    kid_refs, o_ref = refs[:pp], refs[pp]
    for j in range(pp):
        o_ref[0, :, j * LANES:(j + 1) * LANES] = _idx_scores(qif_ref[0], wcol_ref[0], kid_refs[j][0, 0], t_new)


def _sample_idx(page_table, qif, wcol, cache_k_idx_t, layer, pp, t_new):
    bd, n_pages = page_table.shape
    page = cache_k_idx_t.shape[3]
    nc = n_pages // pp
    per_b = lambda b, c, pt: (b, 0, 0)
    page_specs = [pl.BlockSpec((1, 1, IDX_DIM, page),
                               lambda b, c, pt, j=j: (layer, pt[b, c * pp + j], 0, 0)) for j in range(pp)]
    return pl.pallas_call(
        functools.partial(_sample_idx_kernel, pp=pp, t_new=t_new),
        grid_spec=pltpu.PrefetchScalarGridSpec(
            num_scalar_prefetch=1,
            grid=(bd, nc),
            in_specs=[pl.BlockSpec((1, IDX_HEADS * t_new, IDX_DIM), per_b),
                      pl.BlockSpec((1, IDX_HEADS * t_new, 1), per_b)] + page_specs,
            out_specs=pl.BlockSpec((1, t_new, pp * page), lambda b, c, pt: (b, 0, c)),
        ),
        out_shape=jax.ShapeDtypeStruct((bd, t_new, n_pages * page), F32),
        compiler_params=_cparams(("parallel", "arbitrary")),
        name="sample_idx",
    )(page_table, qif, wcol, *([cache_k_idx_t] * pp))


def _softmax_step(state, lg, vs, guard):
    m, l, acc = state
    m_new = jnp.maximum(m, jnp.max(lg, axis=1, keepdims=True))
    p = jnp.exp(lg - m_new)
    if guard:
        p = jnp.where(lg > 0.5 * NEG, p, 0.0)
    alpha = jnp.exp(m - m_new)
    w = lg.shape[1] // len(vs)
    pv = _dot(p[:, 0:w].astype(MXU_DT), vs[0])
    for j in range(1, len(vs)):
        pv = pv + _dot(p[:, j * w:(j + 1) * w].astype(MXU_DT), vs[j])
    return (m_new, alpha * l + jnp.sum(p, axis=1, keepdims=True), alpha * acc + pv)


def _sample_mix_kernel(pt_ref, sp_ref, qif_ref, wcol_ref, kin_ref, qa_ref, qb_ref,
                       ba_ref, bb_ref, gsub_ref, lam_ref, kan_ref, van_ref, kbn_ref, vbn_ref, *refs,
                       pp, t_new, topk, n_heads, lam_init):
    ka_refs, va_refs = refs[0:pp], refs[pp:2 * pp]
    kb_refs, vb_refs = refs[2 * pp:3 * pp], refs[3 * pp:4 * pp]
    oa_ref, ob_ref = refs[4 * pp], refs[4 * pp + 1]
    key_sc, selx_sc, ex_sc, ma_sc, la_sc, acca_sc, mb_sc, lb_sc, accb_sc = refs[4 * pp + 2:]
    c = pl.program_id(1)
    nc = pl.num_programs(1)
    past = sp_ref.shape[2]
    flat = n_heads * LANES
    ra = n_heads * t_new
    scale_a = HEAD_DIM ** -0.5
    scale_b = D_HALF ** -0.5

    def expand_rows(x):
        return jnp.broadcast_to(x[:, None, :], (t_new, n_heads, x.shape[1])).reshape(ra, x.shape[1])

    @pl.when(c == 0)
    def _():
        s_new = _idx_scores(qif_ref[0], wcol_ref[0], kin_ref[0], t_new)
        qi_ = lax.broadcasted_iota(jnp.int32, (t_new, LANES), 0)
        kj_ = lax.broadcasted_iota(jnp.int32, (t_new, LANES), 1)
        key_sc[:, 0:past] = _sort_key(sp_ref[0])
        key_sc[:, past:past + LANES] = _sort_key(jnp.where(kj_ <= qi_, s_new, NEG))
        thr = _kth_largest_key(lambda cd: _count(key_sc[...] >= cd, 1), (t_new, 1), topk)
        cnt_ge = _count(key_sc[...] >= thr, 1)
        selx_sc[...] = expand_rows(jnp.where(key_sc[...] >= thr, 1.0, 0.0)).astype(selx_sc.dtype)
        tied = jnp.max(jnp.where(cnt_ge > topk, 1.0, 0.0))

        @pl.when(tied > 0.0)
        def _():
            need = jnp.float32(topk) - _count(key_sc[...] > thr, 1)
            tri = _tri(LANES, False)

            def chunk(ci, carry):
                cs_ = pl.multiple_of(ci * LANES, LANES)
                kc = key_sc[:, pl.ds(cs_, LANES)]
                eq = kc == thr
                cs = _dot(jnp.where(eq, 1.0, 0.0).astype(MXU_DT), tri) + carry
                keep = (kc > thr) | (eq & (cs <= need))
                selx_sc[:, pl.ds(cs_, LANES)] = expand_rows(jnp.where(keep, 1.0, 0.0)).astype(selx_sc.dtype)
                return cs[:, LANES - 1:LANES]

            lax.fori_loop(0, past // LANES + 1, chunk, jnp.zeros((t_new, 1), F32))

        kk = lax.broadcasted_iota(jnp.int32, (LANES, flat), 0)
        ll = lax.broadcasted_iota(jnp.int32, (LANES, flat), 1)
        ex_sc[...] = jnp.where(ll // n_heads == kk, 1.0, 0.0).astype(ex_sc.dtype)
        ma_sc[...] = jnp.full_like(ma_sc, NEG)
        la_sc[...] = jnp.zeros_like(la_sc)
        acca_sc[...] = jnp.zeros_like(acca_sc)
        mb_sc[...] = jnp.full_like(mb_sc, NEG)
        lb_sc[...] = jnp.zeros_like(lb_sc)
        accb_sc[...] = jnp.zeros_like(accb_sc)

    qa = qa_ref[0]
    qb8 = qb_ref[0]
    lane = lax.broadcasted_iota(jnp.int32, (ra, HEAD_DIM), 1)
    zero = jnp.zeros_like(qb8)
    qb = jnp.concatenate([jnp.where(lane < D_HALF, qb8, zero), jnp.where(lane >= D_HALF, qb8, zero)], axis=0)

    def flat_page(ref4):
        x = ref4[0, 0]
        return x.reshape(x.shape[0] * x.shape[1], x.shape[2]).astype(MXU_DT)

    def pages_step(st_a, st_b, kas, vas, kbs, vbs, key_offs, tiles):
        lga, lgb = [], []
        for kaf, kbf, key_off, tile in zip(kas, kbs, key_offs, tiles):
            sel = _dot(selx_sc[:, pl.ds(key_off, LANES)], ex_sc[...])
            lga.append(jnp.where(sel > 0.5, _dot_nt(qa, kaf) * scale_a + ba_ref[tile], NEG))
            lgb.append(_dot_nt(qb, kbf) * scale_b + bb_ref[tile])
        st_a = _softmax_step(st_a, jnp.concatenate(lga, axis=1), vas, True)
        st_b = _softmax_step(st_b, jnp.concatenate(lgb, axis=1), vbs, False)
        return st_a, st_b

    tiles = [0] * (pp - 1) + [jnp.where(c == nc - 1, 1, 0)]
    st_a, st_b = pages_step(
        (ma_sc[...], la_sc[...], acca_sc[...]), (mb_sc[...], lb_sc[...], accb_sc[...]),
        [flat_page(r) for r in ka_refs], [flat_page(r) for r in va_refs],
        [flat_page(r) for r in kb_refs], [flat_page(r) for r in vb_refs],
        [pl.multiple_of((c * pp + j) * LANES, LANES) for j in range(pp)], tiles)
    ma_sc[...], la_sc[...], acca_sc[...] = st_a
    mb_sc[...], lb_sc[...], accb_sc[...] = st_b

    @pl.when(c == nc - 1)
    def _():
        def new_page(ref3):
            return ref3[0].astype(MXU_DT)
        fa, fb = pages_step((ma_sc[...], la_sc[...], acca_sc[...]), (mb_sc[...], lb_sc[...], accb_sc[...]),
                            [new_page(kan_ref)], [new_page(van_ref)], [new_page(kbn_ref)], [new_page(vbn_ref)],
                            [past], [2])
        oa_ref[0] = (fa[2] / fa[1]).astype(oa_ref.dtype)
        lam = _lambda(lam_ref, lam_init)
        o = fb[2] / fb[1]
        o = o[:ra] - lam * o[ra:]
        ob_ref[0] = (_rms(o, gsub_ref[...]) * (1.0 - lam_init)).astype(ob_ref.dtype)


def _sample_mix(page_table, s_past, qif, wcol, ki_new_t, qa8, qb8, bias_a, bias_b, g_sub_l, lam_vecs,
                new_pages, caches, layer, pp, t_new, topk, lam_init):
    bd, n_pages = page_table.shape
    cache_k_a = caches[0]
    page, n_heads = cache_k_a.shape[2], cache_k_a.shape[3]
    past = n_pages * page
    nc = n_pages // pp
    ra = n_heads * t_new
    flat = n_heads * LANES
    per_b = lambda b, c, pt: (b, 0, 0)
    const3 = lambda b, c, pt: (0, 0, 0)
    const2 = lambda b, c, pt: (0, 0)

    def page_specs():
        return [pl.BlockSpec((1, 1, page, n_heads, HEAD_DIM),
                             lambda b, c, pt, j=j: (layer, pt[b, c * pp + j], 0, 0, 0)) for j in range(pp)]

    in_specs = [
        pl.BlockSpec((1, t_new, past), per_b),
        pl.BlockSpec((1, IDX_HEADS * t_new, IDX_DIM), per_b),
        pl.BlockSpec((1, IDX_HEADS * t_new, 1), per_b),
        pl.BlockSpec((1, IDX_DIM, LANES), per_b),
        pl.BlockSpec((1, ra, HEAD_DIM), per_b),
        pl.BlockSpec((1, ra, HEAD_DIM), per_b),
        pl.BlockSpec((3, ra, flat), const3),
        pl.BlockSpec((3, 2 * ra, flat), const3),
        pl.BlockSpec((1, HEAD_DIM), const2),
        pl.BlockSpec((4, D_HALF), const2),
    ] + [pl.BlockSpec((1, flat, HEAD_DIM), per_b)] * 4 + page_specs() + page_specs() + page_specs() + page_specs()
    out_spec = pl.BlockSpec((1, ra, HEAD_DIM), per_b)
    scratch = [
        pltpu.VMEM((t_new, past + LANES), jnp.int32),
        pltpu.VMEM((ra, past + LANES), MXU_DT),
        pltpu.VMEM((LANES, flat), MXU_DT),
        pltpu.VMEM((ra, 1), F32), pltpu.VMEM((ra, 1), F32), pltpu.VMEM((ra, HEAD_DIM), F32),
        pltpu.VMEM((2 * ra, 1), F32), pltpu.VMEM((2 * ra, 1), F32), pltpu.VMEM((2 * ra, HEAD_DIM), F32),
    ]
    ck_a, cv_a, ck_b, cv_b = caches
    return pl.pallas_call(
        functools.partial(_sample_mix_kernel, pp=pp, t_new=t_new, topk=topk, n_heads=n_heads,
                          lam_init=lam_init),
        grid_spec=pltpu.PrefetchScalarGridSpec(
            num_scalar_prefetch=1, grid=(bd, nc), in_specs=in_specs,
            out_specs=(out_spec, out_spec), scratch_shapes=scratch),
        out_shape=(jax.ShapeDtypeStruct((bd, ra, HEAD_DIM), MXU_DT),) * 2,
        compiler_params=_cparams(("parallel", "arbitrary")),
        name="sample_mix",
    )(page_table, s_past, qif, wcol, ki_new_t, qa8, qb8, bias_a, bias_b, g_sub_l, lam_vecs,
      *new_pages, *([ck_a] * pp), *([cv_a] * pp), *([ck_b] * pp), *([cv_b] * pp))


def _largest_divisor(n, cap):
    d = min(n, cap)
    while n % d:
        d -= 1
    return d


def _tiles(m_prompt, seq, n_pages, d_ff):
    return dict(
        tm_in=_largest_divisor(m_prompt, 256),
        tm_out=_largest_divisor(m_prompt, 512),
        tm_mlp=_largest_divisor(m_prompt, 512),
        tf=_largest_divisor(d_ff, 1024),
        tq=_largest_divisor(seq, 256),
        pp=_largest_divisor(n_pages, 8),
        pp_idx=_largest_divisor(n_pages, 32),
        heads_per_loop_a=8,
        heads_per_loop_b=4,
    )


def kernel(x_prompt, x_sample, cache_k_a, cache_v_a, cache_k_idx, cache_k_b, cache_v_b, page_table, w_in, w_out, w_up, w_down, g_attn, g_mlp, g_sub, lam_q1, lam_k1, lam_q2, lam_k2, rel_bias, g_final):
    batch, seq, d_model = x_prompt.shape
    bd, t_new, _ = x_sample.shape
    depth = w_in.shape[0]
    n_pool, page, h_a = cache_k_a.shape[1], cache_k_a.shape[2], cache_k_a.shape[3]
    h_b = cache_k_b.shape[3]
    n_pages = page_table.shape[1]
    past = n_pages * page
    wa, wb = h_a * HEAD_DIM, h_b * HEAD_DIM
    d_ff = w_up.shape[2]
    assert h_a == h_b and wa == 1024 and page == LANES and d_model == wa + wb
    topk_p = min(TOPK_MAX, seq // 4)
    topk_s = min(TOPK_MAX, (past + t_new) // 4)
    assert topk_s <= past and t_new <= LANES
    mp, ms = batch * seq, bd * t_new
    t = _tiles(mp, seq, n_pages, d_ff)
    tq, pp = t["tq"], t["pp"]
    assert tq >= topk_p

    sizes = (wa, wa, wa, IDX_HEADS * IDX_DIM, IDX_DIM, IDX_HEADS, wb, wb, wb)
    offs = np.concatenate([[0], np.cumsum(sizes)])
    seg = lambda n: w_in[:, :, int(offs[n]):int(offs[n + 1])]
    w_main = jnp.concatenate([seg(0), seg(1), seg(2), seg(3), seg(6), seg(7), seg(8)], axis=-1).astype(MXU_DT)
    pad = jnp.zeros(w_in.shape[:2] + (SMALL_W - 2 * IDX_DIM - IDX_HEADS,), w_in.dtype)
    w_small = jnp.concatenate([seg(4), seg(4), seg(5), pad], axis=-1).astype(MXU_DT)
    w_out_c, w_up_c, w_down_c = w_out.astype(MXU_DT), w_up.astype(MXU_DT), w_down.astype(MXU_DT)
    lam_all = jnp.stack([lam_q1, lam_k1, lam_q2, lam_k2], axis=1)
    g_attn, g_mlp = g_attn.reshape(depth, 1, d_model), g_mlp.reshape(depth, 1, d_model)
    g_final2 = g_final.reshape(1, d_model)
    cache_k_idx_t = jnp.swapaxes(cache_k_idx, 2, 3)

    kr = jnp.arange(tq, dtype=jnp.int32)[:, None]
    qr = jnp.arange(tq, dtype=jnp.int32)[None, :]
    d_prev, d_diag = qr + tq - kr, qr - kr
    assert tq >= MAX_DISTANCE
    bkt_p = jnp.concatenate([jnp.full((tq, tq), N_BUCKETS - 1, jnp.int32),
                             _t5_bucket(d_prev),
                             jnp.where(d_diag >= 0, _t5_bucket(d_diag), -1)], axis=0)
    bias_p = _bias_lookup(rel_bias, bkt_p).reshape(h_a + h_b, 3, tq, tq)
    bias_pa, bias_pb = bias_p[:h_a], bias_p[h_a:]

    qpos = past + jnp.arange(t_new, dtype=jnp.int32)[:, None]
    kl = jnp.arange(LANES, dtype=jnp.int32)[None, :]
    d_last = qpos - (past - LANES + kl)
    d_new = qpos - (past + kl)
    bkt_s = jnp.concatenate([jnp.full((t_new, LANES), N_BUCKETS - 1, jnp.int32),
                             _t5_bucket(d_last),
                             jnp.where((d_new >= 0) & (kl < t_new), _t5_bucket(d_new), -1)], axis=0)
    assert past - LANES >= 0 and t_new + LANES >= MAX_DISTANCE
    bias_s = _bias_lookup(rel_bias, bkt_s).reshape(h_a + h_b, 3, t_new, LANES)

    def flat_tiles(bs, reps):
        hh = bs.shape[0]
        x = jnp.transpose(bs, (1, 2, 0, 3))
        eye = jnp.eye(hh, dtype=bool)[None, None, :, None, :]
        x = jnp.where(eye, x[..., None], NEG).reshape(3, t_new * hh, LANES * hh)
        return jnp.concatenate([x] * reps, axis=1)

    bias_sa, bias_sb = flat_tiles(bias_s[:h_a], 1), flat_tiles(bias_s[h_a:], 2)

    hp = x_prompt.reshape(mp, d_model)
    hs = x_sample.reshape(ms, d_model)
    rows_p, rows_s = [], []
    for l in range(depth):
        lam_init = 0.8 - 0.6 * math.exp(-0.3 * l)
        last = l == depth - 1
        qa, qi, qb, ka, va, kb, vb, kah, vah, kbh, vbh, small = _inproj(hp, g_attn, w_main, w_small, l, t["tm_in"])
        oa, ob = _prompt_attention(qa.T, qi.T, qb.T, kah, vah.T, kbh, vbh.T, small, small.T, bias_pa, bias_pb,
                                   g_sub[l].reshape(HEAD_DIM, 1), lam_all[l], batch, seq, tq, topk_p, lam_init,
                                   t["heads_per_loop_a"], t["heads_per_loop_b"])
        hp = _outproj(oa, ob, w_out_c, hp, l, t["tm_out"])
        hp = _mlp(hp, g_mlp, w_up_c, w_down_c, g_final2, l, t["tm_mlp"], t["tf"], last)
        rows_p.append((ka, va, small[:, :IDX_DIM], kb, vb))
        qa, qi, qb, ka, va, kb, vb, _, _, _, _, small = _inproj(hs, g_attn, w_main, w_small, l, ms)
        ki = small[:, :IDX_DIM]
        qif = qi.reshape(bd, t_new * IDX_HEADS, IDX_DIM)
        wcol = (small[:, 2 * IDX_DIM:2 * IDX_DIM + IDX_HEADS] * IDX_W_SCALE).reshape(bd, t_new * IDX_HEADS, 1)
        s_past = _sample_idx(page_table, qif, wcol, cache_k_idx_t, l, t["pp_idx"], t_new)

        def as_page(x, width):
            x = x.reshape(bd, t_new, width // HEAD_DIM, HEAD_DIM)
            x = jnp.pad(x, ((0, 0), (0, LANES - t_new), (0, 0), (0, 0)))
            return x.reshape(bd, LANES * (width // HEAD_DIM), HEAD_DIM)

        ki_new_t = jnp.swapaxes(jnp.pad(ki.reshape(bd, t_new, IDX_DIM), ((0, 0), (0, LANES - t_new), (0, 0))), 1, 2)
        oa, ob = _sample_mix(page_table, s_past, qif, wcol, ki_new_t,
                             qa.reshape(bd, t_new * h_a, HEAD_DIM), qb.reshape(bd, t_new * h_b, HEAD_DIM),
                             bias_sa, bias_sb, g_sub[l:l + 1], lam_all[l],
                             (as_page(ka, wa), as_page(va, wa), as_page(kb, wb), as_page(vb, wb)),
                             (cache_k_a, cache_v_a, cache_k_b, cache_v_b), l, pp, t_new, topk_s, lam_init)
        hs = _outproj(oa.reshape(ms, wa), ob.reshape(ms, wb), w_out_c, hs, l, ms)
        hs = _mlp(hs, g_mlp, w_up_c, w_down_c, g_final2, l, ms, t["tf"], last)
        rows_s.append((ka, va, ki, kb, vb))

    def stack(rows, n, lead):
        outs = []
        for idx in range(5):
            a = jnp.stack([r_[idx] for r_ in rows])
            outs.append(a.reshape((depth,) + lead + ((IDX_DIM,) if idx == 2 else (n, HEAD_DIM))))
        return outs

    k_a_p, v_a_p, kidx_p, k_b_p, v_b_p = stack(rows_p, h_a, (batch, seq))
    k_a_s, v_a_s, kidx_s, k_b_s, v_b_s = stack(rows_s, h_a, (bd, t_new))
    y_prompt = hp.reshape(batch, seq, d_model)
    y_sample = hs.reshape(bd, t_new, d_model)
    return (y_prompt, y_sample, k_a_p, v_a_p, kidx_p, k_b_p, v_b_p, k_a_s, v_a_s, kidx_s, k_b_s, v_b_s)
```

```python
import functools
import math

import jax
import jax.numpy as jnp
import numpy as np
from jax import lax
from jax.experimental import pallas as pl
from jax.experimental.pallas import tpu as pltpu

HEAD_DIM = 128
D_HALF = HEAD_DIM // 2
IDX_HEADS = 16
IDX_DIM = 64
TOPK_MAX = 256
N_BUCKETS = 32
MAX_DISTANCE = 128
EPS = 1e-6
NEG = -1e30
INT_MIN = -(2 ** 31)

F32 = jnp.float32
MXU_DT = jnp.bfloat16

V7X_VMEM_BYTES = 64 * 1024 * 1024
VMEM_LIMIT = V7X_VMEM_BYTES - 8 * 1024 * 1024
LANES = 128

N_MAIN = 7
SMALL_W = 256
IDX_W_SCALE = IDX_HEADS ** -0.5 * IDX_DIM ** -0.5
BIAS_BLK = MAX_DISTANCE


def _cparams(sem):
    return pltpu.CompilerParams(dimension_semantics=sem, vmem_limit_bytes=VMEM_LIMIT)


def _dot(a, b):
    return jnp.dot(a, b, preferred_element_type=F32)


def _dot_nt(a, b):
    return lax.dot_general(a, b, (((1,), (1,)), ((), ())), preferred_element_type=F32)


def _rms(x, g):
    return x * lax.rsqrt(jnp.mean(x * x, axis=-1, keepdims=True) + EPS) * g


def _lambda(lam_ref, lam_init):
    lq = lam_ref[...]
    return (jnp.exp(jnp.sum(lq[0:1] * lq[1:2], axis=1, keepdims=True))
            - jnp.exp(jnp.sum(lq[2:3] * lq[3:4], axis=1, keepdims=True)) + lam_init)


def _inproj_kernel(x_ref, g_ref, w_ref, ws_ref, ka_all, va_all, kb_all, vb_all,
                   qa_ref, qi_ref, qb_ref, ka_ref, va_ref, kb_ref, vb_ref,
                   kah_ref, vah_ref, kbh_ref, vbh_ref, sm_ref, xn_sc):
    del ka_all, va_all, kb_all, vb_all
    j = pl.program_id(1)

    @pl.when(j == 0)
    def _():
        xn_sc[...] = _rms(x_ref[...], g_ref[...]).astype(xn_sc.dtype)
        sm_ref[...] = _dot(xn_sc[...], ws_ref[...])

    z = _dot(xn_sc[...], w_ref[...])
    q_out = {0: qa_ref, 3: qi_ref, 4: qb_ref}
    kv_out = {1: (ka_ref, kah_ref), 2: (va_ref, vah_ref), 5: (kb_ref, kbh_ref), 6: (vb_ref, vbh_ref)}
    for jj in range(N_MAIN):
        @pl.when(j == jj)
        def _(jj=jj):
            if jj in q_out:
                q_out[jj][...] = z.astype(q_out[jj].dtype)
            else:
                full, half = kv_out[jj]
                full[...] = z
                half[...] = z.astype(half.dtype)


def _inproj(x, g, w_main, w_small, kv_all, layer, tm):
    M, D = x.shape
    W = HEAD_DIM * 8
    grid = (M // tm, N_MAIN)
    row = lambda i, j: (i, 0)
    o_lo = jax.ShapeDtypeStruct((M, W), MXU_DT)
    o_all = jax.ShapeDtypeStruct(kv_all[0].shape, F32)
    outs = (o_lo, o_lo, o_lo, o_all, o_all, o_all, o_all, o_lo, o_lo, o_lo, o_lo,
            jax.ShapeDtypeStruct((M, SMALL_W), F32))
    lo_spec = pl.BlockSpec((tm, W), row)
    all_spec = pl.BlockSpec((None, tm, W), lambda i, j: (layer, i, 0))
    out_specs = tuple([lo_spec] * 3 + [all_spec] * 4 + [lo_spec] * 4 + [pl.BlockSpec((tm, SMALL_W), row)])
    n_in = 4
    return pl.pallas_call(
        _inproj_kernel,
        grid=grid,
        in_specs=[
            pl.BlockSpec((tm, D), row),
            pl.BlockSpec((None, 1, D), lambda i, j: (layer, 0, 0)),
            pl.BlockSpec((None, D, W), lambda i, j: (layer, 0, j)),
            pl.BlockSpec((None, D, SMALL_W), lambda i, j: (layer, 0, 0)),
        ] + [pl.BlockSpec(memory_space=pl.ANY)] * 4,
        out_specs=out_specs,
        out_shape=outs,
        input_output_aliases={n_in + n: 3 + n for n in range(4)},
        scratch_shapes=[pltpu.VMEM((tm, D), MXU_DT)],
        compiler_params=_cparams(("parallel", "arbitrary")),
        name="inproj",
    )(x, g, w_main, w_small, *kv_all)


def _bias_kernel(tab_ref, bkt_ref, o_ref):
    h = pl.program_id(0)
    bkt = bkt_ref[...]
    acc = jnp.full(bkt.shape, NEG, F32)
    for b in range(N_BUCKETS):
        acc = jnp.where(bkt == b, tab_ref[b, h], acc)
    o_ref[0] = acc


def _bias_lookup(rel_bias, bucket):
    R, C = bucket.shape
    H = rel_bias.shape[1]
    return pl.pallas_call(
        _bias_kernel,
        grid=(H,),
        in_specs=[pl.BlockSpec(memory_space=pltpu.SMEM),
                  pl.BlockSpec((R, C), lambda h: (0, 0))],
        out_specs=pl.BlockSpec((1, R, C), lambda h: (h, 0, 0)),
        out_shape=jax.ShapeDtypeStruct((H, R, C), F32),
        compiler_params=_cparams(("arbitrary",)),
        name="bias_lookup",
    )(rel_bias, bucket)


def _t5_bucket(dist):
    dist = jnp.maximum(dist, 0)
    max_exact = N_BUCKETS // 2
    d = jnp.maximum(dist, 1).astype(F32)
    large = max_exact + (jnp.log(d / max_exact) / math.log(MAX_DISTANCE / max_exact)
                         * (N_BUCKETS - max_exact)).astype(jnp.int32)
    large = jnp.minimum(large, N_BUCKETS - 1)
    return jnp.where(dist < max_exact, dist, large)


def _sort_key(s):
    b = lax.bitcast_convert_type(s, jnp.int32)
    return jnp.where(b < 0, b ^ jnp.int32(0x7FFFFFFF), b)


def _kth_largest_key(count_ge, shape, k):
    kk = jnp.float32(k)
    zero = jnp.zeros(shape, jnp.int32)
    t0 = jnp.where(count_ge(zero) >= kk, zero, jnp.full(shape, INT_MIN, jnp.int32))

    def body(it, t):
        cand = t + lax.shift_left(jnp.int32(1), jnp.int32(30) - it)
        return jnp.where(count_ge(cand) >= kk, cand, t)

    return lax.fori_loop(0, 31, body, t0)


def _count(mask, axis):
    return jnp.sum(jnp.where(mask, 1.0, 0.0), axis=axis, keepdims=True)


def _tri(n, lower):
    r = lax.broadcasted_iota(jnp.int32, (n, n), 0)
    c = lax.broadcasted_iota(jnp.int32, (n, n), 1)
    return jnp.where((r >= c) if lower else (r <= c), 1.0, 0.0).astype(MXU_DT)


def _head_cols(h):
    return slice(h * HEAD_DIM, (h + 1) * HEAD_DIM)


def _add_position_bias(lg, mode, near_ref, far_c, h, tq):
    if mode == "far":
        return lg + far_c
    nb = tq // BIAS_BLK
    cols = []
    for c0 in range(0, lg.shape[1], BIAS_BLK):
        b = (c0 % tq) // BIAS_BLK
        blocks = []
        for a in range(nb):
            delta = b - a + (nb if mode == "prev" else 0)
            if delta < 0:
                blocks.append(jnp.full((BIAS_BLK, BIAS_BLK), NEG, F32))
                continue
            blk = lg[a * BIAS_BLK:(a + 1) * BIAS_BLK, c0:c0 + BIAS_BLK]
            blocks.append(blk + (near_ref[h, delta] if delta < 2 else far_c))
        cols.append(jnp.concatenate(blocks, axis=0))
    return jnp.concatenate(cols, axis=1)


def _flash_heads_t(q_ts, heads, k_ref, vt_ref, i, tq, scale, near_ref, far_cs, extra_fn, guard, acc_sc):
    R = q_ts[0].shape[1]
    reps = R // tq
    for h in heads:
        acc_sc[h] = jnp.zeros((HEAD_DIM, R), F32)

    def step(mode, kb, carry):
        ks = pl.multiple_of(kb * tq, tq)
        extra = None if extra_fn is None else extra_fn(kb)
        out = []
        for g, h in enumerate(heads):
            m, l = carry[g]
            hs = _head_cols(h)
            lg = _dot(k_ref[pl.ds(ks, tq), hs], q_ts[g]) * scale
            lg = _add_position_bias(lg, mode, near_ref, far_cs[g], h, tq)
            if extra is not None:
                lg = jnp.concatenate([lg[:, r * tq:(r + 1) * tq] + extra for r in range(reps)], axis=1)
            m_new = jnp.maximum(m, jnp.max(lg, axis=0, keepdims=True))
            p = jnp.exp(lg - m_new)
            if guard:
                p = jnp.where(lg > 0.5 * NEG, p, 0.0)
            alpha = jnp.exp(m - m_new)
            l = alpha * l + jnp.sum(p, axis=0, keepdims=True)
            acc_sc[h] = alpha * acc_sc[h] + _dot(vt_ref[hs, pl.ds(ks, tq)], p.astype(MXU_DT))
            out.append((m_new, l))
        return tuple(out)

    carry = tuple((jnp.full((1, R), NEG, F32), jnp.zeros((1, R), F32)) for _ in heads)
    carry = lax.fori_loop(0, jnp.maximum(i - 1, 0), functools.partial(step, "far"), carry)
    carry = lax.fori_loop(0, jnp.minimum(i, 1), lambda _, c: step("prev", i - 1, c), carry)
    carry = step("diag", i, carry)
    return [ml[1] for ml in carry]


def _dsa_prompt_kernel(rb_ref, qat_ref, qit_ref, smt_ref, smk_ref, k_ref, vt_ref, bt_ref, o_ref,
                       key_sc, sel_sc, acc_sc, *, tq, topk, n_heads, heads_per_loop, head0):
    i = pl.program_id(1)
    tk = tq
    n_kb = i + 1
    w_t = smt_ref[2 * IDX_DIM:2 * IDX_DIM + IDX_HEADS, :] * IDX_W_SCALE
    sub = lax.broadcasted_iota(jnp.int32, (2 * IDX_DIM, tq), 0)
    qpos = i * tq + lax.broadcasted_iota(jnp.int32, (tk, tq), 1)
    krel = lax.broadcasted_iota(jnp.int32, (tk, tq), 0)
    few = i * tq + lax.broadcasted_iota(jnp.int32, (1, tq), 1) < topk

    def causal(kb):
        return kb * tk + krel <= qpos

    def tile(kb):
        return pl.ds(pl.multiple_of(kb * tk, tk), tk)

    def score_tile(kb, carry):
        ki2 = smk_ref[tile(kb), 0:2 * IDX_DIM].astype(MXU_DT)
        s = jnp.zeros((tk, tq), F32)
        for pr in range(IDX_HEADS // 2):
            qp = qit_ref[pr * 128:(pr + 1) * 128, :]
            zero = jnp.zeros_like(qp)
            d_lo = _dot(ki2, jnp.where(sub < IDX_DIM, qp, zero))
            d_hi = _dot(ki2, jnp.where(sub >= IDX_DIM, qp, zero))
            s = s + w_t[2 * pr:2 * pr + 1, :] * jnp.maximum(d_lo, 0.0)
            s = s + w_t[2 * pr + 1:2 * pr + 2, :] * jnp.maximum(d_hi, 0.0)
        key_sc[tile(kb), :] = _sort_key(jnp.where(causal(kb), s, NEG))
        return carry

    lax.fori_loop(0, n_kb, score_tile, 0)

    def count_where(pred):
        def inner(kb, c):
            return c + _count(pred(key_sc[tile(kb), :]), 0)
        return lax.fori_loop(0, n_kb, inner, jnp.zeros((1, tq), F32))

    thr = _kth_largest_key(lambda cd: count_where(lambda kt: kt >= cd), (1, tq), topk)
    cnt_ge = count_where(lambda kt: kt >= thr)

    def sel_tile(kb, carry):
        keep = (key_sc[tile(kb), :] >= thr) | few
        sel_sc[tile(kb), :] = jnp.where(causal(kb) & keep, 0.0, NEG)
        return carry

    lax.fori_loop(0, n_kb, sel_tile, 0)
    tied = jnp.max(jnp.where((cnt_ge > topk) & jnp.logical_not(few), 1.0, 0.0))

    @pl.when(tied > 0.0)
    def _():
        need = jnp.float32(topk) - count_where(lambda kt: kt > thr)
        tri = _tri(tk, True)

        def tie_tile(kb, carry):
            kt = key_sc[tile(kb), :]
            eq = kt == thr
            cs = _dot(tri, jnp.where(eq, 1.0, 0.0).astype(MXU_DT)) + carry
            keep = (kt > thr) | few | (eq & (cs <= need))
            sel_sc[tile(kb), :] = jnp.where(causal(kb) & keep, 0.0, NEG)
            return cs[tk - 1:tk, :]

        lax.fori_loop(0, n_kb, tie_tile, jnp.zeros((1, tq), F32))

    for h0 in range(0, n_heads, heads_per_loop):
        heads = list(range(h0, h0 + heads_per_loop))
        ls = _flash_heads_t([qat_ref[_head_cols(h), :] for h in heads], heads, k_ref, vt_ref, i, tq,
                            HEAD_DIM ** -0.5, bt_ref, [rb_ref[N_BUCKETS - 1, head0 + h] for h in heads],
                            lambda kb: sel_sc[tile(kb), :], True, acc_sc)
        for h, l in zip(heads, ls):
            o_ref[:, _head_cols(h)] = jnp.transpose(acc_sc[h] / l).astype(o_ref.dtype)


def _diff_prompt_kernel(rb_ref, qbt_ref, k_ref, vt_ref, bt_ref, gsub_ref, lam_ref, o_ref, acc_sc,
                        *, tq, n_heads, lam_init, heads_per_loop, head0):
    i = pl.program_id(1)
    lam = _lambda(lam_ref, lam_init)
    sub = lax.broadcasted_iota(jnp.int32, (HEAD_DIM, tq), 0)

    def two_maps(h):
        qh = qbt_ref[_head_cols(h), :]
        zero = jnp.zeros_like(qh)
        return jnp.concatenate([jnp.where(sub < D_HALF, qh, zero), jnp.where(sub >= D_HALF, qh, zero)], axis=1)

    for h0 in range(0, n_heads, heads_per_loop):
        heads = list(range(h0, h0 + heads_per_loop))
        ls = _flash_heads_t([two_maps(h) for h in heads], heads, k_ref, vt_ref, i, tq,
                            D_HALF ** -0.5, bt_ref, [rb_ref[N_BUCKETS - 1, head0 + h] for h in heads],
                            None, False, acc_sc)
        for h, l in zip(heads, ls):
            o = acc_sc[h] / l
            o = o[:, :tq] - lam * o[:, tq:]
            o = o * lax.rsqrt(jnp.mean(o * o, axis=0, keepdims=True) + EPS) * gsub_ref[...]
            o_ref[:, _head_cols(h)] = jnp.transpose(o * (1.0 - lam_init)).astype(o_ref.dtype)


def _prompt_attention(rel_bias, qa_t, qi_t, qb_t, kah, vah_t, kbh, vbh_t, small, small_t, bias_near,
                      g_sub_col, lam_vecs, batch, seq, tq, topk, lam_init, hpl_a, hpl_b):
    W, M = qa_t.shape
    nq = seq // tq
    n_heads = W // HEAD_DIM
    grid = (batch, nq)
    qrow = lambda b, i: (b * nq + i, 0)
    qcol = lambda b, i: (0, b * nq + i)
    brow = lambda b, i: (b, 0)
    bcol = lambda b, i: (0, b)
    smem = pl.BlockSpec(memory_space=pltpu.SMEM)
    near_a = pl.BlockSpec((n_heads, 2, BIAS_BLK, BIAS_BLK), lambda b, i: (0, 0, 0, 0))
    near_b = pl.BlockSpec((n_heads, 2, BIAS_BLK, BIAS_BLK), lambda b, i: (1, 0, 0, 0))
    oa = pl.pallas_call(
        functools.partial(_dsa_prompt_kernel, tq=tq, topk=topk, n_heads=n_heads, heads_per_loop=hpl_a,
                          head0=0),
        grid=grid,
        in_specs=[
            smem,
            pl.BlockSpec((W, tq), qcol),
            pl.BlockSpec((W, tq), qcol),
            pl.BlockSpec((SMALL_W, tq), qcol),
            pl.BlockSpec((seq, SMALL_W), brow),
            pl.BlockSpec((seq, W), brow),
            pl.BlockSpec((W, seq), bcol),
            near_a,
        ],
        out_specs=pl.BlockSpec((tq, W), qrow),
        out_shape=jax.ShapeDtypeStruct((M, W), MXU_DT),
        scratch_shapes=[pltpu.VMEM((seq, tq), jnp.int32), pltpu.VMEM((seq, tq), F32),
                        pltpu.VMEM((n_heads, HEAD_DIM, tq), F32)],
        compiler_params=_cparams(("parallel", "arbitrary")),
        name="dsa_prompt",
    )(rel_bias, qa_t, qi_t, small_t, small, kah, vah_t, bias_near)
    ob = pl.pallas_call(
        functools.partial(_diff_prompt_kernel, tq=tq, n_heads=n_heads, lam_init=lam_init,
                          heads_per_loop=hpl_b, head0=n_heads),
        grid=grid,
        in_specs=[
            smem,
            pl.BlockSpec((W, tq), qcol),
            pl.BlockSpec((seq, W), brow),
            pl.BlockSpec((W, seq), bcol),
            near_b,
            pl.BlockSpec((HEAD_DIM, 1), lambda b, i: (0, 0)),
            pl.BlockSpec((4, D_HALF), lambda b, i: (0, 0)),
        ],
        out_specs=pl.BlockSpec((tq, W), qrow),
        out_shape=jax.ShapeDtypeStruct((M, W), MXU_DT),
        scratch_shapes=[pltpu.VMEM((n_heads, HEAD_DIM, 2 * tq), F32)],
        compiler_params=_cparams(("parallel", "arbitrary")),
        name="diff_prompt",
    )(rel_bias, qb_t, kbh, vbh_t, bias_near, g_sub_col, lam_vecs)
    return oa, ob


def _outproj_kernel(oa_ref, ob_ref, w_ref, res_ref, o_ref):
    wa = oa_ref.shape[1]
    o_ref[...] = res_ref[...] + _dot(oa_ref[...], w_ref[0:wa, :]) + _dot(ob_ref[...], w_ref[wa:, :])


def _outproj(oa, ob, w_out, res, layer, tm):
    M, D = res.shape
    wa = oa.shape[1]
    row = lambda i: (i, 0)
    return pl.pallas_call(
        _outproj_kernel,
        grid=(M // tm,),
        in_specs=[pl.BlockSpec((tm, wa), row), pl.BlockSpec((tm, wa), row),
                  pl.BlockSpec((None, 2 * wa, D), lambda i: (layer, 0, 0)),
                  pl.BlockSpec((tm, D), row)],
        out_specs=pl.BlockSpec((tm, D), row),
        out_shape=jax.ShapeDtypeStruct((M, D), F32),
        compiler_params=_cparams(("parallel",)),
        name="outproj",
    )(oa, ob, w_out, res)


def _mlp_kernel(x_ref, g_ref, wu_ref, wd_ref, gf_ref, o_ref, xn_sc, acc_sc, *, final_norm):
    f = pl.program_id(1)

    @pl.when(f == 0)
    def _():
        xn_sc[...] = _rms(x_ref[...], g_ref[...]).astype(xn_sc.dtype)
        acc_sc[...] = jnp.zeros_like(acc_sc)

    h = jnp.maximum(_dot(xn_sc[...], wu_ref[...]), 0.0)
    acc_sc[...] += _dot((h * h).astype(MXU_DT), wd_ref[...])

    @pl.when(f == pl.num_programs(1) - 1)
    def _():
        y = x_ref[...] + acc_sc[...]
        o_ref[...] = _rms(y, gf_ref[...]) if final_norm else y


def _mlp(x, g_mlp, w_up, w_down, g_final, layer, tm, tf, final_norm):
    M, D = x.shape
    d_ff = w_up.shape[2]
    row = lambda i, f: (i, 0)
    return pl.pallas_call(
        functools.partial(_mlp_kernel, final_norm=final_norm),
        grid=(M // tm, d_ff // tf),
        in_specs=[pl.BlockSpec((tm, D), row),
                  pl.BlockSpec((None, 1, D), lambda i, f: (layer, 0, 0)),
                  pl.BlockSpec((None, D, tf), lambda i, f: (layer, 0, f)),
                  pl.BlockSpec((None, tf, D), lambda i, f: (layer, f, 0)),
                  pl.BlockSpec((1, D), lambda i, f: (0, 0))],
        out_specs=pl.BlockSpec((tm, D), row),
        out_shape=jax.ShapeDtypeStruct((M, D), F32),
        scratch_shapes=[pltpu.VMEM((tm, D), MXU_DT), pltpu.VMEM((tm, D), F32)],
        compiler_params=_cparams(("parallel", "arbitrary")),
        name="mlp",
    )(x, g_mlp, w_up, w_down, g_final)


def _idx_scores(qif, wcol, kpage_t, t_new):
    d = _dot(qif, kpage_t.astype(MXU_DT))
    x = jnp.maximum(d, 0.0) * wcol
    return jnp.sum(x.reshape(t_new, IDX_HEADS, x.shape[1]), axis=1)


def _sample_idx_kernel(pt_ref, qif_ref, wcol_ref, *refs, pp, t_new):
    kid_refs, o_ref = refs[:pp], refs[pp]
    for j in range(pp):
        o_ref[0, :, j * LANES:(j + 1) * LANES] = _idx_scores(qif_ref[0], wcol_ref[0], kid_refs[j][0, 0], t_new)


def _sample_idx(page_table, qif, wcol, cache_k_idx_t, layer, pp, t_new):
    bd, n_pages = page_table.shape
    page = cache_k_idx_t.shape[3]
    nc = n_pages // pp
    per_b = lambda b, c, pt: (b, 0, 0)
    page_specs = [pl.BlockSpec((1, 1, IDX_DIM, page),
                               lambda b, c, pt, j=j: (layer, pt[b, c * pp + j], 0, 0)) for j in range(pp)]
    return pl.pallas_call(
        functools.partial(_sample_idx_kernel, pp=pp, t_new=t_new),
        grid_spec=pltpu.PrefetchScalarGridSpec(
            num_scalar_prefetch=1,
            grid=(bd, nc),
            in_specs=[pl.BlockSpec((1, IDX_HEADS * t_new, IDX_DIM), per_b),
                      pl.BlockSpec((1, IDX_HEADS * t_new, 1), per_b)] + page_specs,
            out_specs=pl.BlockSpec((1, t_new, pp * page), lambda b, c, pt: (b, 0, c)),
        ),
        out_shape=jax.ShapeDtypeStruct((bd, t_new, n_pages * page), F32),
        compiler_params=_cparams(("parallel", "arbitrary")),
        name="sample_idx",
    )(page_table, qif, wcol, *([cache_k_idx_t] * pp))


def _softmax_step(state, lg, vs, guard):
    m, l, acc = state
    m_new = jnp.maximum(m, jnp.max(lg, axis=1, keepdims=True))
    p = jnp.exp(lg - m_new)
    if guard:
        p = jnp.where(lg > 0.5 * NEG, p, 0.0)
    alpha = jnp.exp(m - m_new)
    w = lg.shape[1] // len(vs)
    pv = _dot(p[:, 0:w].astype(MXU_DT), vs[0])
    for j in range(1, len(vs)):
        pv = pv + _dot(p[:, j * w:(j + 1) * w].astype(MXU_DT), vs[j])
    return (m_new, alpha * l + jnp.sum(p, axis=1, keepdims=True), alpha * acc + pv)


def _sample_mix_kernel(pt_ref, sp_ref, qif_ref, wcol_ref, kin_ref, qa_ref, qb_ref,
                       ba_ref, bb_ref, gsub_ref, lam_ref, kan_ref, van_ref, kbn_ref, vbn_ref, *refs,
                       pp, t_new, topk, n_heads, lam_init):
    ka_refs, va_refs = refs[0:pp], refs[pp:2 * pp]
    kb_refs, vb_refs = refs[2 * pp:3 * pp], refs[3 * pp:4 * pp]
    oa_ref, ob_ref = refs[4 * pp], refs[4 * pp + 1]
    key_sc, selx_sc, ex_sc, ma_sc, la_sc, acca_sc, mb_sc, lb_sc, accb_sc = refs[4 * pp + 2:]
    c = pl.program_id(1)
    nc = pl.num_programs(1)
    past = sp_ref.shape[2]
    flat = n_heads * LANES
    ra = n_heads * t_new
    scale_a = HEAD_DIM ** -0.5
    scale_b = D_HALF ** -0.5

    def expand_rows(x):
        return jnp.broadcast_to(x[:, None, :], (t_new, n_heads, x.shape[1])).reshape(ra, x.shape[1])

    @pl.when(c == 0)
    def _():
        s_new = _idx_scores(qif_ref[0], wcol_ref[0], kin_ref[0], t_new)
        qi_ = lax.broadcasted_iota(jnp.int32, (t_new, LANES), 0)
        kj_ = lax.broadcasted_iota(jnp.int32, (t_new, LANES), 1)
        key_sc[:, 0:past] = _sort_key(sp_ref[0])
        key_sc[:, past:past + LANES] = _sort_key(jnp.where(kj_ <= qi_, s_new, NEG))
        thr = _kth_largest_key(lambda cd: _count(key_sc[...] >= cd, 1), (t_new, 1), topk)
        cnt_ge = _count(key_sc[...] >= thr, 1)
        selx_sc[...] = expand_rows(jnp.where(key_sc[...] >= thr, 1.0, 0.0)).astype(selx_sc.dtype)
        tied = jnp.max(jnp.where(cnt_ge > topk, 1.0, 0.0))

        @pl.when(tied > 0.0)
        def _():
            need = jnp.float32(topk) - _count(key_sc[...] > thr, 1)
            tri = _tri(LANES, False)

            def chunk(ci, carry):
                cs_ = pl.multiple_of(ci * LANES, LANES)
                kc = key_sc[:, pl.ds(cs_, LANES)]
                eq = kc == thr
                cs = _dot(jnp.where(eq, 1.0, 0.0).astype(MXU_DT), tri) + carry
                keep = (kc > thr) | (eq & (cs <= need))
                selx_sc[:, pl.ds(cs_, LANES)] = expand_rows(jnp.where(keep, 1.0, 0.0)).astype(selx_sc.dtype)
                return cs[:, LANES - 1:LANES]

            lax.fori_loop(0, past // LANES + 1, chunk, jnp.zeros((t_new, 1), F32))

        kk = lax.broadcasted_iota(jnp.int32, (LANES, flat), 0)
        ll = lax.broadcasted_iota(jnp.int32, (LANES, flat), 1)
        ex_sc[...] = jnp.where(ll // n_heads == kk, 1.0, 0.0).astype(ex_sc.dtype)
        ma_sc[...] = jnp.full_like(ma_sc, NEG)
        la_sc[...] = jnp.zeros_like(la_sc)
        acca_sc[...] = jnp.zeros_like(acca_sc)
        mb_sc[...] = jnp.full_like(mb_sc, NEG)
        lb_sc[...] = jnp.zeros_like(lb_sc)
        accb_sc[...] = jnp.zeros_like(accb_sc)

    qa = qa_ref[0]
    qb8 = qb_ref[0]
    lane = lax.broadcasted_iota(jnp.int32, (ra, HEAD_DIM), 1)
    zero = jnp.zeros_like(qb8)
    qb = jnp.concatenate([jnp.where(lane < D_HALF, qb8, zero), jnp.where(lane >= D_HALF, qb8, zero)], axis=0)

    def flat_page(ref4):
        x = ref4[0, 0]
        return x.reshape(x.shape[0] * x.shape[1], x.shape[2]).astype(MXU_DT)

    def pages_step(st_a, st_b, kas, vas, kbs, vbs, key_offs, tiles):
        lga, lgb = [], []
        for kaf, kbf, key_off, tile in zip(kas, kbs, key_offs, tiles):
            sel = _dot(selx_sc[:, pl.ds(key_off, LANES)], ex_sc[...])
            lga.append(jnp.where(sel > 0.5, _dot_nt(qa, kaf) * scale_a + ba_ref[tile], NEG))
            lgb.append(_dot_nt(qb, kbf) * scale_b + bb_ref[tile])
        st_a = _softmax_step(st_a, jnp.concatenate(lga, axis=1), vas, True)
        st_b = _softmax_step(st_b, jnp.concatenate(lgb, axis=1), vbs, False)
        return st_a, st_b

    tiles = [0] * (pp - 1) + [jnp.where(c == nc - 1, 1, 0)]
    st_a, st_b = pages_step(
        (ma_sc[...], la_sc[...], acca_sc[...]), (mb_sc[...], lb_sc[...], accb_sc[...]),
        [flat_page(r) for r in ka_refs], [flat_page(r) for r in va_refs],
        [flat_page(r) for r in kb_refs], [flat_page(r) for r in vb_refs],
        [pl.multiple_of((c * pp + j) * LANES, LANES) for j in range(pp)], tiles)
    ma_sc[...], la_sc[...], acca_sc[...] = st_a
    mb_sc[...], lb_sc[...], accb_sc[...] = st_b

    @pl.when(c == nc - 1)
    def _():
        def new_page(ref3):
            return ref3[0].astype(MXU_DT)
        fa, fb = pages_step((ma_sc[...], la_sc[...], acca_sc[...]), (mb_sc[...], lb_sc[...], accb_sc[...]),
                            [new_page(kan_ref)], [new_page(van_ref)], [new_page(kbn_ref)], [new_page(vbn_ref)],
                            [past], [2])
        oa_ref[0] = (fa[2] / fa[1]).astype(oa_ref.dtype)
        lam = _lambda(lam_ref, lam_init)
        o = fb[2] / fb[1]
        o = o[:ra] - lam * o[ra:]
        ob_ref[0] = (_rms(o, gsub_ref[...]) * (1.0 - lam_init)).astype(ob_ref.dtype)


def _sample_mix(page_table, s_past, qif, wcol, ki_new_t, qa8, qb8, bias_a, bias_b, g_sub_l, lam_vecs,
                new_pages, caches, layer, pp, t_new, topk, lam_init):
    bd, n_pages = page_table.shape
    cache_k_a = caches[0]
    page, n_heads = cache_k_a.shape[2], cache_k_a.shape[3]
    past = n_pages * page
    nc = n_pages // pp
    ra = n_heads * t_new
    flat = n_heads * LANES
    per_b = lambda b, c, pt: (b, 0, 0)
    const3 = lambda b, c, pt: (0, 0, 0)
    const2 = lambda b, c, pt: (0, 0)

    def page_specs():
        return [pl.BlockSpec((1, 1, page, n_heads, HEAD_DIM),
                             lambda b, c, pt, j=j: (layer, pt[b, c * pp + j], 0, 0, 0)) for j in range(pp)]

    in_specs = [
        pl.BlockSpec((1, t_new, past), per_b),
        pl.BlockSpec((1, IDX_HEADS * t_new, IDX_DIM), per_b),
        pl.BlockSpec((1, IDX_HEADS * t_new, 1), per_b),
        pl.BlockSpec((1, IDX_DIM, LANES), per_b),
        pl.BlockSpec((1, ra, HEAD_DIM), per_b),
        pl.BlockSpec((1, ra, HEAD_DIM), per_b),
        pl.BlockSpec((3, ra, flat), const3),
        pl.BlockSpec((3, 2 * ra, flat), const3),
        pl.BlockSpec((1, HEAD_DIM), const2),
        pl.BlockSpec((4, D_HALF), const2),
    ] + [pl.BlockSpec((1, flat, HEAD_DIM), per_b)] * 4 + page_specs() + page_specs() + page_specs() + page_specs()
    out_spec = pl.BlockSpec((1, ra, HEAD_DIM), per_b)
    scratch = [
        pltpu.VMEM((t_new, past + LANES), jnp.int32),
        pltpu.VMEM((ra, past + LANES), MXU_DT),
        pltpu.VMEM((LANES, flat), MXU_DT),
        pltpu.VMEM((ra, 1), F32), pltpu.VMEM((ra, 1), F32), pltpu.VMEM((ra, HEAD_DIM), F32),
        pltpu.VMEM((2 * ra, 1), F32), pltpu.VMEM((2 * ra, 1), F32), pltpu.VMEM((2 * ra, HEAD_DIM), F32),
    ]
    ck_a, cv_a, ck_b, cv_b = caches
    return pl.pallas_call(
        functools.partial(_sample_mix_kernel, pp=pp, t_new=t_new, topk=topk, n_heads=n_heads,
                          lam_init=lam_init),
        grid_spec=pltpu.PrefetchScalarGridSpec(
            num_scalar_prefetch=1, grid=(bd, nc), in_specs=in_specs,
            out_specs=(out_spec, out_spec), scratch_shapes=scratch),
        out_shape=(jax.ShapeDtypeStruct((bd, ra, HEAD_DIM), MXU_DT),) * 2,
        compiler_params=_cparams(("parallel", "arbitrary")),
        name="sample_mix",
    )(page_table, s_past, qif, wcol, ki_new_t, qa8, qb8, bias_a, bias_b, g_sub_l, lam_vecs,
      *new_pages, *([ck_a] * pp), *([cv_a] * pp), *([ck_b] * pp), *([cv_b] * pp))


def _largest_divisor(n, cap):
    d = min(n, cap)
    while n % d:
        d -= 1
    return d


def _tiles(m_prompt, seq, n_pages, d_ff):
    return dict(
        tm_in=_largest_divisor(m_prompt, 512),
        tm_out=_largest_divisor(m_prompt, 512),
        tm_mlp=_largest_divisor(m_prompt, 512),
        tf=_largest_divisor(d_ff, 1024),
        tq=_largest_divisor(seq, 512),
        pp=_largest_divisor(n_pages, 8),
        pp_idx=_largest_divisor(n_pages, 32),
        heads_per_loop_a=8,
        heads_per_loop_b=4,
    )


def kernel(x_prompt, x_sample, cache_k_a, cache_v_a, cache_k_idx, cache_k_b, cache_v_b, page_table, w_in, w_out, w_up, w_down, g_attn, g_mlp, g_sub, lam_q1, lam_k1, lam_q2, lam_k2, rel_bias, g_final):
    batch, seq, d_model = x_prompt.shape
    bd, t_new, _ = x_sample.shape
    depth = w_in.shape[0]
    n_pool, page, h_a = cache_k_a.shape[1], cache_k_a.shape[2], cache_k_a.shape[3]
    h_b = cache_k_b.shape[3]
    n_pages = page_table.shape[1]
    past = n_pages * page
    wa, wb = h_a * HEAD_DIM, h_b * HEAD_DIM
    d_ff = w_up.shape[2]
    assert h_a == h_b and wa == 1024 and page == LANES and d_model == wa + wb
    topk_p = min(TOPK_MAX, seq // 4)
    topk_s = min(TOPK_MAX, (past + t_new) // 4)
    assert topk_s <= past and t_new <= LANES
    mp, ms = batch * seq, bd * t_new
    t = _tiles(mp, seq, n_pages, d_ff)
    tq, pp = t["tq"], t["pp"]
    assert tq >= topk_p

    sizes = (wa, wa, wa, IDX_HEADS * IDX_DIM, IDX_DIM, IDX_HEADS, wb, wb, wb)
    offs = np.concatenate([[0], np.cumsum(sizes)])
    w_lo = w_in.astype(MXU_DT)
    cols = lambda n0, n1: w_lo[:, :, int(offs[n0]):int(offs[n1])]
    w_main = jnp.concatenate([cols(0, 4), cols(6, 9)], axis=-1)
    pad = jnp.zeros(w_in.shape[:2] + (SMALL_W - 2 * IDX_DIM - IDX_HEADS,), MXU_DT)
    w_small = jnp.concatenate([cols(4, 5), cols(4, 5), cols(5, 6), pad], axis=-1)
    w_out_c, w_up_c, w_down_c = w_out.astype(MXU_DT), w_up.astype(MXU_DT), w_down.astype(MXU_DT)
    lam_all = jnp.stack([lam_q1, lam_k1, lam_q2, lam_k2], axis=1)
    g_attn, g_mlp = g_attn.reshape(depth, 1, d_model), g_mlp.reshape(depth, 1, d_model)
    g_final2 = g_final.reshape(1, d_model)
    cache_k_idx_t = jnp.swapaxes(cache_k_idx, 2, 3)

    kr = jnp.arange(BIAS_BLK, dtype=jnp.int32)[:, None]
    qr = jnp.arange(BIAS_BLK, dtype=jnp.int32)[None, :]
    d_diag, d_prev = qr - kr, qr + BIAS_BLK - kr
    assert tq % BIAS_BLK == 0
    bkt_p = jnp.concatenate([jnp.where(d_diag >= 0, _t5_bucket(d_diag), -1), _t5_bucket(d_prev)], axis=0)
    bias_near = _bias_lookup(rel_bias, bkt_p).reshape(h_a + h_b, 2, BIAS_BLK, BIAS_BLK)

    qpos = past + jnp.arange(t_new, dtype=jnp.int32)[:, None]
    kl = jnp.arange(LANES, dtype=jnp.int32)[None, :]
    d_last = qpos - (past - LANES + kl)
    d_new = qpos - (past + kl)
    bkt_s = jnp.concatenate([jnp.full((t_new, LANES), N_BUCKETS - 1, jnp.int32),
                             _t5_bucket(d_last),
                             jnp.where((d_new >= 0) & (kl < t_new), _t5_bucket(d_new), -1)], axis=0)
    assert past - LANES >= 0 and t_new + LANES >= MAX_DISTANCE
    bias_s = _bias_lookup(rel_bias, bkt_s).reshape(h_a + h_b, 3, t_new, LANES)

    def flat_tiles(bs, reps):
        hh = bs.shape[0]
        x = jnp.transpose(bs, (1, 2, 0, 3))
        eye = jnp.eye(hh, dtype=bool)[None, None, :, None, :]
        x = jnp.where(eye, x[..., None], NEG).reshape(3, t_new * hh, LANES * hh)
        return jnp.concatenate([x] * reps, axis=1)

    bias_sa, bias_sb = flat_tiles(bias_s[:h_a], 1), flat_tiles(bias_s[h_a:], 2)

    hp = x_prompt.reshape(mp, d_model)
    hs = x_sample.reshape(ms, d_model)
    kidx_p, kidx_s = [], []
    kv_p = [jnp.zeros((depth, mp, wa), F32) for _ in range(4)]
    kv_s = [jnp.zeros((depth, ms, wa), F32) for _ in range(4)]
    for l in range(depth):
        lam_init = 0.8 - 0.6 * math.exp(-0.3 * l)
        last = l == depth - 1
        qa, qi, qb, *kv_p, kah, vah, kbh, vbh, small = _inproj(hp, g_attn, w_main, w_small, kv_p, l, t["tm_in"])
        oa, ob = _prompt_attention(rel_bias, qa.T, qi.T, qb.T, kah, vah.T, kbh, vbh.T, small, small.T, bias_near,
                                   g_sub[l].reshape(HEAD_DIM, 1), lam_all[l], batch, seq, tq, topk_p, lam_init,
                                   t["heads_per_loop_a"], t["heads_per_loop_b"])
        hp = _outproj(oa, ob, w_out_c, hp, l, t["tm_out"])
        hp = _mlp(hp, g_mlp, w_up_c, w_down_c, g_final2, l, t["tm_mlp"], t["tf"], last)
        kidx_p.append(small[:, :IDX_DIM])
        qa, qi, qb, *kv_s, _, _, _, _, small = _inproj(hs, g_attn, w_main, w_small, kv_s, l, ms)
        ka, va, kb, vb = (a[l] for a in kv_s)
        ki = small[:, :IDX_DIM]
        kidx_s.append(ki)
        qif = qi.reshape(bd, t_new * IDX_HEADS, IDX_DIM)
        wcol = (small[:, 2 * IDX_DIM:2 * IDX_DIM + IDX_HEADS] * IDX_W_SCALE).reshape(bd, t_new * IDX_HEADS, 1)
        s_past = _sample_idx(page_table, qif, wcol, cache_k_idx_t, l, t["pp_idx"], t_new)

        def as_page(x, width):
            x = x.reshape(bd, t_new, width // HEAD_DIM, HEAD_DIM)
            x = jnp.pad(x, ((0, 0), (0, LANES - t_new), (0, 0), (0, 0)))
            return x.reshape(bd, LANES * (width // HEAD_DIM), HEAD_DIM)

        ki_new_t = jnp.swapaxes(jnp.pad(ki.reshape(bd, t_new, IDX_DIM), ((0, 0), (0, LANES - t_new), (0, 0))), 1, 2)
        oa, ob = _sample_mix(page_table, s_past, qif, wcol, ki_new_t,
                             qa.reshape(bd, t_new * h_a, HEAD_DIM), qb.reshape(bd, t_new * h_b, HEAD_DIM),
                             bias_sa, bias_sb, g_sub[l:l + 1], lam_all[l],
                             (as_page(ka, wa), as_page(va, wa), as_page(kb, wb), as_page(vb, wb)),
                             (cache_k_a, cache_v_a, cache_k_b, cache_v_b), l, pp, t_new, topk_s, lam_init)
        hs = _outproj(oa.reshape(ms, wa), ob.reshape(ms, wb), w_out_c, hs, l, ms)
        hs = _mlp(hs, g_mlp, w_up_c, w_down_c, g_final2, l, ms, t["tf"], last)

    def finish(kv_all, kidx, lead):
        k_a, v_a, k_b, v_b = (a.reshape((depth,) + lead + (h_a, HEAD_DIM)) for a in kv_all)
        return k_a, v_a, jnp.stack(kidx).reshape((depth,) + lead + (IDX_DIM,)), k_b, v_b

    y_prompt = hp.reshape(batch, seq, d_model)
    y_sample = hs.reshape(bd, t_new, d_model)
    return (y_prompt, y_sample) + finish(kv_p, kidx_p, (batch, seq)) + finish(kv_s, kidx_s, (bd, t_new))
```
